```python
import jax, jax.numpy as jnp
from jax import lax
import numpy as np

D_MODEL = 2048
BATCH = 4
SEQ = 2048
DEPTH = 4
DEC_BATCH = 128
DEC_SEQ = 1
PAST_LEN = 8192
PAGE_SIZE = 128

MLA_HEADS = 8
Q_LORA = 512
KV_LORA = 256
NOPE_DIM = 128
ROPE_DIM = 64
MLA_V_DIM = 128
MLA_WIDTH = MLA_HEADS * MLA_V_DIM
ROPE_BASE = 10000.0
MLA_SCALE = (NOPE_DIM + ROPE_DIM) ** -0.5
GM_GROUPS = 4
GM_CH = 128
GM_WIDTH = GM_GROUPS * GM_CH
CHUNK = 128
FOX_HEADS = 8
FOX_KV_HEADS = 2
FOX_GROUP = FOX_HEADS // FOX_KV_HEADS
FOX_HEAD_DIM = 64
FOX_WIDTH = FOX_HEADS * FOX_HEAD_DIM
FOX_KV_WIDTH = FOX_KV_HEADS * FOX_HEAD_DIM
FOX_SCALE = FOX_HEAD_DIM ** -0.5
FORGET_BIAS_MIN = 2.0
FORGET_BIAS_MAX = 8.0
MIX_WIDTH = MLA_WIDTH + GM_WIDTH + FOX_WIDTH
Q_BLOCK = 128
N_EXPERTS = 32
N_GROUPS = 4
EXPERTS_PER_GROUP = N_EXPERTS // N_GROUPS
TOP_K = 2
D_EXPERT = 512
PLE_DIM = 256
ALPHA = (2 * DEPTH) ** 0.25
BETA = (8 * DEPTH) ** -0.25
LN_EPS = 1e-5
RMS_EPS = 1e-6
NEG_INF = -1e30

_IN_SIZES = (Q_LORA, KV_LORA, ROPE_DIM, GM_WIDTH, GM_WIDTH, FOX_WIDTH, FOX_KV_WIDTH, FOX_KV_WIDTH, FOX_HEADS)
IN_COLS = sum(_IN_SIZES)
_IN_SPLITS = tuple(int(s) for s in np.cumsum(_IN_SIZES)[:-1])

kernel_name = 'hymba_mla_gmlp_fox_moe_decode_step'


def layer_norm(x, g, b):
    xf = x.astype(jnp.float32)
    mu = jnp.mean(xf, axis=-1, keepdims=True)
    var = jnp.mean(jnp.square(xf - mu), axis=-1, keepdims=True)
    return ((xf - mu) * lax.rsqrt(var + LN_EPS) * g + b).astype(x.dtype)


def rms_norm(x, g):
    xf = x.astype(jnp.float32)
    return (xf * lax.rsqrt(jnp.mean(xf * xf, axis=-1, keepdims=True) + RMS_EPS) * g).astype(x.dtype)


def rope(x, pos):
    half = x.shape[-1] // 2
    inv = ROPE_BASE ** (-jnp.arange(half, dtype=jnp.float32) / half)
    ang = pos.astype(jnp.float32)[:, None] * inv[None, :]
    cos = jnp.cos(ang)[:, None, :]
    sin = jnp.sin(ang)[:, None, :]
    xf = x.astype(jnp.float32)
    x1, x2 = xf[..., :half], xf[..., half:]
    return jnp.concatenate([x1 * cos - x2 * sin, x2 * cos + x1 * sin], axis=-1).astype(x.dtype)


def causal_softmax(sc, q_pos, k_pos):
    sc = jnp.where(k_pos[None, :] <= q_pos[:, None], sc, NEG_INF)
    return jax.nn.softmax(sc, axis=-1)


def gather_pages(pool, page_table):
    g = pool[page_table]
    return g.reshape((g.shape[0], g.shape[1] * g.shape[2]) + g.shape[3:])


def mixer_inputs(x, pos, w_in, q_norm_g, kv_norm_g, w_uq, gm_ln_g, gm_ln_b, fox_bf):
    B, T, _ = x.shape
    z = x @ w_in
    q_lat, kv_lat, k_r, u, v, fq, fk, fv, fl = jnp.split(z, _IN_SPLITS, axis=-1)
    q = (rms_norm(q_lat, q_norm_g) @ w_uq).reshape(B, T, MLA_HEADS, NOPE_DIM + ROPE_DIM)
    q_nope = q[..., :NOPE_DIM]
    q_rope = rope(q[..., NOPE_DIM:], pos)
    c_kv = rms_norm(kv_lat, kv_norm_g)
    k_rope = rope(k_r[:, :, None, :], pos)[:, :, 0, :]
    u = jax.nn.gelu(u)
    v = layer_norm(jax.nn.gelu(v), gm_ln_g, gm_ln_b)
    fq = fq.reshape(B, T, FOX_KV_HEADS, FOX_GROUP, FOX_HEAD_DIM)
    fk = fk.reshape(B, T, FOX_KV_HEADS, FOX_HEAD_DIM)
    fv = fv.reshape(B, T, FOX_KV_HEADS, FOX_HEAD_DIM)
    log_f = jax.nn.log_sigmoid(fl.astype(jnp.float32) + fox_bf.astype(jnp.float32))
    return (q_nope, q_rope, c_kv, k_rope, u, v, fq, fk, fv, log_f)


def mla_prompt(q_nope, q_rope, c_kv, k_rope, w_ukv):
    B, T = q_nope.shape[:2]
    kv = (c_kv @ w_ukv).reshape(B, T, MLA_HEADS, NOPE_DIM + MLA_V_DIM)
    k_nope, v = kv[..., :NOPE_DIM], kv[..., NOPE_DIM:]
    pos = jnp.arange(T)
    outs = []
    for s in range(0, T, Q_BLOCK):
        e = min(s + Q_BLOCK, T)
        sc = (jnp.einsum('bqhn,bkhn->bhqk', q_nope[:, s:e], k_nope[:, :e])
              + jnp.einsum('bqhr,bkr->bhqk', q_rope[:, s:e], k_rope[:, :e])).astype(jnp.float32) * MLA_SCALE
        p = causal_softmax(sc, pos[s:e], pos[:e])
        outs.append(jnp.einsum('bhqk,bkhv->bqhv', p.astype(v.dtype), v[:, :e]))
    return jnp.concatenate(outs, axis=1).reshape(B, T, MLA_WIDTH)


def mla_sample(q_nope, q_rope, c_kv_new, k_rope_new, ckv_past, krope_past, w_ukv):
    B, Tq = q_nope.shape[:2]
    past = ckv_past.shape[1]
    ckv = jnp.concatenate([ckv_past, c_kv_new], axis=1)
    kr = jnp.concatenate([krope_past, k_rope_new], axis=1)
    w = w_ukv.reshape(KV_LORA, MLA_HEADS, NOPE_DIM + MLA_V_DIM)
    w_uk, w_uv = w[..., :NOPE_DIM], w[..., NOPE_DIM:]
    q_abs = jnp.einsum('bqhn,chn->bqhc', q_nope, w_uk)
    sc = (jnp.einsum('bqhc,bkc->bhqk', q_abs, ckv)
          + jnp.einsum('bqhr,bkr->bhqk', q_rope, kr)).astype(jnp.float32) * MLA_SCALE
    p = causal_softmax(sc, past + jnp.arange(Tq), jnp.arange(past + Tq))
    o_lat = jnp.einsum('bhqk,bkc->bqhc', p.astype(ckv.dtype), ckv)
    return jnp.einsum('bqhc,chv->bqhv', o_lat, w_uv).reshape(B, Tq, MLA_WIDTH)


def gmlp_mix(u, v, ws, bs):
    B, T, _ = v.shape
    n = -(-T // CHUNK)
    vc = jnp.pad(v, ((0, 0), (0, n * CHUNK - T), (0, 0))).reshape(B, n, CHUNK, GM_GROUPS, GM_CH)
    w = jnp.where(jnp.tril(jnp.ones((CHUNK, CHUNK), dtype=bool)), ws, 0)
    s = jnp.einsum('gts,bnsgc->bntgc', w, vc) + bs.T[:, :, None]
    return u * s.reshape(B, n * CHUNK, GM_WIDTH)[:, :T]


def fox_attend(q, k, v, c_q, c_k, q_pos, k_pos):
    B, Tq = q.shape[:2]
    sc = jnp.einsum('bqngd,bknd->bngqk', q, k).astype(jnp.float32) * FOX_SCALE
    sc = sc + jnp.transpose(c_q, (0, 2, 3, 1))[..., None] - jnp.transpose(c_k, (0, 2, 3, 1))[..., None, :]
    p = causal_softmax(sc, q_pos, k_pos)
    o = jnp.einsum('bngqk,bknd->bqngd', p.astype(v.dtype), v)
    return o.reshape(B, Tq, FOX_WIDTH)


def fox_prompt(q, k, v, log_f):
    B, T = q.shape[:2]
    c = jnp.cumsum(log_f, axis=1).reshape(B, T, FOX_KV_HEADS, FOX_GROUP)
    pos = jnp.arange(T)
    outs = []
    for s in range(0, T, Q_BLOCK):
        e = min(s + Q_BLOCK, T)
        outs.append(fox_attend(q[:, s:e], k[:, :e], v[:, :e], c[:, s:e], c[:, :e], pos[s:e], pos[:e]))
    return jnp.concatenate(outs, axis=1)


def fox_sample(q, k_new, v_new, logf_new, k_past, v_past, logf_past):
    B, Tq = q.shape[:2]
    past = k_past.shape[1]
    k = jnp.concatenate([k_past, k_new], axis=1)
    v = jnp.concatenate([v_past, v_new], axis=1)
    lf = jnp.concatenate([logf_past.astype(jnp.float32), logf_new], axis=1)
    c = jnp.cumsum(lf, axis=1).reshape(B, past + Tq, FOX_KV_HEADS, FOX_GROUP)
    return fox_attend(q, k, v, c[:, past:], c, past + jnp.arange(Tq), jnp.arange(past + Tq))


def moe(h, router_w, router_b, w_gate, w_up, w_down):
    shp = h.shape
    t = h.reshape(-1, D_MODEL)
    scores = jax.nn.sigmoid((t @ router_w).astype(jnp.float32))
    sel = scores + router_b.astype(jnp.float32)
    grp_score = lax.top_k(sel.reshape(-1, N_GROUPS, EXPERTS_PER_GROUP), TOP_K)[0].sum(-1)
    g_idx = jnp.argmax(grp_score, axis=-1)
    in_grp = (jnp.arange(N_EXPERTS) // EXPERTS_PER_GROUP)[None, :] == g_idx[:, None]
    _, e_idx = lax.top_k(jnp.where(in_grp, sel, NEG_INF), TOP_K)
    gw = jnp.take_along_axis(scores, e_idx, axis=-1)
    gw = gw / jnp.sum(gw, axis=-1, keepdims=True)
    combine = jnp.sum(jax.nn.one_hot(e_idx, N_EXPERTS, dtype=jnp.float32) * gw[..., None], axis=1)
    a = jax.nn.silu(jnp.einsum('td,edf->tef', t, w_gate)) * jnp.einsum('td,edf->tef', t, w_up)
    a = a * combine[:, :, None].astype(a.dtype)
    return jnp.einsum('tef,efd->td', a, w_down).reshape(shp)


def finish_layer(x, o_cat, p_l, w_out, ln1_g, ln1_b, ln2_g, ln2_b, ln3_g, ln3_b,
                 router_w, router_b, w_e_gate, w_e_up, w_e_down, w_pe, w_pg):
    x = layer_norm(ALPHA * x + o_cat @ w_out, ln1_g, ln1_b)
    x = layer_norm(ALPHA * x + moe(x, router_w, router_b, w_e_gate, w_e_up, w_e_down), ln2_g, ln2_b)
    e = (p_l @ w_pe) * jax.nn.sigmoid(x @ w_pg)
    return layer_norm(ALPHA * x + e, ln3_g, ln3_b)


def setup_inputs(seed: int = 0) -> dict:
    key = jax.random.key(seed)
    ks = iter(jax.random.split(key, 48))
    f32 = jnp.float32

    def nrm(shape, scale=1.0):
        return jax.random.normal(next(ks), shape, f32) * scale

    n_pages = PAST_LEN // PAGE_SIZE
    n_used = DEC_BATCH * n_pages
    n_pool = n_used + max(1, n_used // 4)
    page_table = jax.random.permutation(next(ks), n_pool)[:n_used].reshape(DEC_BATCH, n_pages).astype(jnp.int32)
    forget_bias = jnp.linspace(FORGET_BIAS_MIN, FORGET_BIAS_MAX, FOX_HEADS, dtype=f32)
    return {
        'x_prompt': nrm((BATCH, SEQ, D_MODEL)),
        'x_sample': nrm((DEC_BATCH, DEC_SEQ, D_MODEL)),
        'cache_mla_kv': nrm((DEPTH, n_pool, PAGE_SIZE, KV_LORA)),
        'cache_mla_krope': nrm((DEPTH, n_pool, PAGE_SIZE, ROPE_DIM)),
        'cache_fox_k': nrm((DEPTH, n_pool, PAGE_SIZE, FOX_KV_HEADS, FOX_HEAD_DIM)),
        'cache_fox_v': nrm((DEPTH, n_pool, PAGE_SIZE, FOX_KV_HEADS, FOX_HEAD_DIM)),
        'cache_fox_logf': jax.nn.log_sigmoid(forget_bias + nrm((DEPTH, n_pool, PAGE_SIZE, FOX_HEADS))),
        'page_table': page_table,
        'p_prompt': nrm((DEPTH, BATCH, SEQ, PLE_DIM)),
        'p_sample': nrm((DEPTH, DEC_BATCH, DEC_SEQ, PLE_DIM)),
        'w_in': nrm((DEPTH, D_MODEL, IN_COLS), D_MODEL ** -0.5),
        'q_norm_g': 1.0 + nrm((DEPTH, Q_LORA), 0.02),
        'kv_norm_g': 1.0 + nrm((DEPTH, KV_LORA), 0.02),
        'w_uq': nrm((DEPTH, Q_LORA, MLA_HEADS * (NOPE_DIM + ROPE_DIM)), Q_LORA ** -0.5),
        'w_ukv': nrm((DEPTH, KV_LORA, MLA_HEADS * (NOPE_DIM + MLA_V_DIM)), KV_LORA ** -0.5),
        'gmlp_ln_g': 1.0 + nrm((DEPTH, GM_WIDTH), 0.02),
        'gmlp_ln_b': nrm((DEPTH, GM_WIDTH), 0.02),
        'gmlp_ws': nrm((DEPTH, GM_GROUPS, CHUNK, CHUNK), CHUNK ** -0.5),
        'gmlp_bs': 1.0 + nrm((DEPTH, GM_GROUPS, CHUNK), 0.02),
        'fox_bf': forget_bias[None, :] + nrm((DEPTH, FOX_HEADS), 0.1),
        'w_out': nrm((DEPTH, MIX_WIDTH, D_MODEL), MIX_WIDTH ** -0.5 * BETA),
        'ln1_g': 1.0 + nrm((DEPTH, D_MODEL), 0.02),
        'ln1_b': nrm((DEPTH, D_MODEL), 0.02),
        'ln2_g': 1.0 + nrm((DEPTH, D_MODEL), 0.02),
        'ln2_b': nrm((DEPTH, D_MODEL), 0.02),
        'ln3_g': 1.0 + nrm((DEPTH, D_MODEL), 0.02),
        'ln3_b': nrm((DEPTH, D_MODEL), 0.02),
        'router_w': nrm((D_MODEL, N_EXPERTS), D_MODEL ** -0.5),
        'router_b': nrm((N_EXPERTS,), 0.01),
        'w_e_gate': nrm((DEPTH, N_EXPERTS, D_MODEL, D_EXPERT), D_MODEL ** -0.5),
        'w_e_up': nrm((DEPTH, N_EXPERTS, D_MODEL, D_EXPERT), D_MODEL ** -0.5),
        'w_e_down': nrm((DEPTH, N_EXPERTS, D_EXPERT, D_MODEL), D_EXPERT ** -0.5 * BETA),
        'w_pe': nrm((DEPTH, PLE_DIM, D_MODEL), PLE_DIM ** -0.5 * BETA),
        'w_pg': nrm((DEPTH, D_MODEL, D_MODEL), D_MODEL ** -0.5),
    }


def reference(x_prompt, x_sample, cache_mla_kv, cache_mla_krope, cache_fox_k, cache_fox_v, cache_fox_logf,
              page_table, p_prompt, p_sample, w_in, q_norm_g, kv_norm_g, w_uq, w_ukv, gmlp_ln_g, gmlp_ln_b,
              gmlp_ws, gmlp_bs, fox_bf, w_out, ln1_g, ln1_b, ln2_g, ln2_b, ln3_g, ln3_b, router_w, router_b,
              w_e_gate, w_e_up, w_e_down, w_pe, w_pg):
    pos_p = jnp.arange(x_prompt.shape[1])
    pos_s = PAST_LEN + jnp.arange(x_sample.shape[1])
    hp, hs = x_prompt, x_sample
    p_ckv, p_kr, p_fk, p_fv, p_lf = [], [], [], [], []
    s_ckv, s_kr, s_fk, s_fv, s_lf, s_gv = [], [], [], [], [], []
    for l in range(DEPTH):
        (qn, qr, ckv, kr, u, v, fq, fk, fv, lf) = mixer_inputs(
            hp, pos_p, w_in[l], q_norm_g[l], kv_norm_g[l], w_uq[l], gmlp_ln_g[l], gmlp_ln_b[l], fox_bf[l])
        o_cat = jnp.concatenate([mla_prompt(qn, qr, ckv, kr, w_ukv[l]),
                                 gmlp_mix(u, v, gmlp_ws[l], gmlp_bs[l]),
                                 fox_prompt(fq, fk, fv, lf)], axis=-1)
        hp = finish_layer(hp, o_cat, p_prompt[l], w_out[l], ln1_g[l], ln1_b[l], ln2_g[l], ln2_b[l],
                          ln3_g[l], ln3_b[l], router_w, router_b, w_e_gate[l], w_e_up[l], w_e_down[l],
                          w_pe[l], w_pg[l])
        p_ckv.append(ckv); p_kr.append(kr); p_fk.append(fk); p_fv.append(fv); p_lf.append(lf)
        (qn, qr, ckv, kr, u, v, fq, fk, fv, lf) = mixer_inputs(
            hs, pos_s, w_in[l], q_norm_g[l], kv_norm_g[l], w_uq[l], gmlp_ln_g[l], gmlp_ln_b[l], fox_bf[l])
        o_mla = mla_sample(qn, qr, ckv, kr, gather_pages(cache_mla_kv[l], page_table),
                           gather_pages(cache_mla_krope[l], page_table), w_ukv[l])
        o_fox = fox_sample(fq, fk, fv, lf, gather_pages(cache_fox_k[l], page_table),
                           gather_pages(cache_fox_v[l], page_table), gather_pages(cache_fox_logf[l], page_table))
        o_cat = jnp.concatenate([o_mla, gmlp_mix(u, v, gmlp_ws[l], gmlp_bs[l]), o_fox], axis=-1)
        hs = finish_layer(hs, o_cat, p_sample[l], w_out[l], ln1_g[l], ln1_b[l], ln2_g[l], ln2_b[l],
                          ln3_g[l], ln3_b[l], router_w, router_b, w_e_gate[l], w_e_up[l], w_e_down[l],
                          w_pe[l], w_pg[l])
        s_ckv.append(ckv); s_kr.append(kr); s_fk.append(fk); s_fv.append(fv); s_lf.append(lf); s_gv.append(v)
    return (hp, hs,
            jnp.stack(p_ckv), jnp.stack(p_kr), jnp.stack(p_fk), jnp.stack(p_fv), jnp.stack(p_lf),
            jnp.stack(s_ckv), jnp.stack(s_kr), jnp.stack(s_fk), jnp.stack(s_fv), jnp.stack(s_lf),
            jnp.stack(s_gv))
```

```python
import functools

import jax
import jax.numpy as jnp
import numpy as np
from jax import lax
from jax.experimental import pallas as pl
from jax.experimental.pallas import tpu as pltpu

F32 = jnp.float32
BF16 = jnp.bfloat16
I32 = jnp.int32

MLA_HEADS = 8
Q_LORA = 512
KV_LORA = 256
NOPE_DIM = 128
ROPE_DIM = 64
MLA_V_DIM = 128
ROPE_BASE = 10000.0
MLA_SCALE = (NOPE_DIM + ROPE_DIM) ** -0.5
GM_GROUPS = 4
GM_CH = 128
GM_WIDTH = GM_GROUPS * GM_CH
CHUNK = 128
FOX_HEADS = 8
FOX_KV_HEADS = 2
FOX_GROUP = FOX_HEADS // FOX_KV_HEADS
FOX_HEAD_DIM = 64
FOX_WIDTH = FOX_HEADS * FOX_HEAD_DIM
FOX_KV_WIDTH = FOX_KV_HEADS * FOX_HEAD_DIM
FOX_SCALE = FOX_HEAD_DIM ** -0.5
N_EXPERTS = 32
N_GROUPS = 4
EXPERTS_PER_GROUP = N_EXPERTS // N_GROUPS
D_EXPERT = 512
LN_EPS = 1e-5
RMS_EPS = 1e-6
NEG_INF = -1e30

LANES = 128
SUBLANES = 8
VMEM_LIMIT = 56 * 1024 * 1024

ROW_TILE_CAP = 640
ATT_BLOCK = 256
MOE_TILE = 256
DMA_WINDOW = 64

C_QLAT = 0
C_KVLAT = C_QLAT + Q_LORA
C_KR = C_KVLAT + KV_LORA
C_U = C_KR + LANES
C_V = C_U + GM_WIDTH
C_FQ = C_V + GM_WIDTH
C_FK = C_FQ + FOX_WIDTH
C_FV = C_FK + FOX_KV_WIDTH
C_FL = C_FV + FOX_KV_WIDTH
IN_COLS_P = C_FL + LANES
Q_HEAD_COLS = 2 * LANES


def _cparams(sem, **kw):
    return pltpu.CompilerParams(dimension_semantics=sem, vmem_limit_bytes=VMEM_LIMIT, **kw)


def _dot(a, b):
    return jnp.dot(a, b, preferred_element_type=F32)


def _dot_nt(a, b):
    return lax.dot_general(a, b, (((1,), (1,)), ((), ())), preferred_element_type=F32)


def _layer_norm(x, g, b):
    mu = jnp.mean(x, axis=-1, keepdims=True)
    xc = x - mu
    var = jnp.mean(xc * xc, axis=-1, keepdims=True)
    return xc * lax.rsqrt(var + LN_EPS) * g + b


def _rms_norm(x, g):
    return x * lax.rsqrt(jnp.mean(x * x, axis=-1, keepdims=True) + RMS_EPS) * g


def _gelu_tanh(x):
    return 0.5 * x * (1.0 + jnp.tanh(0.7978845608028654 * (x + 0.044715 * (x * x * x))))


def _log_sigmoid(x):
    return jnp.minimum(x, 0.0) - jnp.log1p(jnp.exp(-jnp.abs(x)))


def _sigmoid(x):
    return 1.0 / (1.0 + jnp.exp(-x))


def _resident(shape):
    nd = len(shape)
    return pl.BlockSpec(shape, lambda *_: (0,) * nd, pipeline_mode=pl.Buffered(1))


def _inproj_kernel(n_prompt_chunks, tm,
                   hb_ref, win_ref, qg_ref, kvg_ref, wuq_ref, wukv_ref, cos_ref, sin_ref,
                   gg_ref, gb_ref, bf_ref, gw_ref, gbias_ref,
                   q_ref, ckv_ref, kr_ref, kvu_ref, gv_ref, go_ref, fq3_ref, fk_ref, fv_ref,
                   fk3_ref, fv3_ref, lf_ref):
    hb = hb_ref[...]

    def z(c0, width):
        return _dot(hb, win_ref[:, c0:c0 + width])

    cos = cos_ref[...]
    sin = sin_ref[...]
    lane = lax.broadcasted_iota(I32, (tm, LANES), 1)
    low_half = lane < (LANES // 2)

    def rope(blk):
        return blk * cos + pltpu.roll(blk, ROPE_DIM // 2, 1) * sin

    qn = _rms_norm(z(C_QLAT, Q_LORA), qg_ref[...])
    qf = _dot(qn.astype(BF16), wuq_ref[...])
    for h in range(MLA_HEADS):
        c0 = h * Q_HEAD_COLS
        q_ref[:, c0:c0 + LANES] = (qf[:, c0:c0 + LANES] * MLA_SCALE).astype(BF16)
        q_ref[:, c0 + LANES:c0 + 2 * LANES] = (rope(qf[:, c0 + LANES:c0 + 2 * LANES]) * MLA_SCALE).astype(BF16)

    ckv = _rms_norm(z(C_KVLAT, KV_LORA), kvg_ref[...])
    ckv_ref[...] = ckv
    kvu_ref[...] = _dot(ckv.astype(BF16), wukv_ref[...]).astype(BF16)
    kr_ref[...] = jnp.where(low_half, rope(z(C_KR, LANES)), 0.0)

    u = _gelu_tanh(z(C_U, GM_WIDTH))
    v = _layer_norm(_gelu_tanh(z(C_V, GM_WIDTH)), gg_ref[...], gb_ref[...])
    gv_ref[...] = v
    nch = tm // CHUNK
    for c in range(nch):
        is_sample = (pl.program_id(0) * nch + c) >= n_prompt_chunks
        r0 = c * CHUNK
        for g in range(GM_GROUPS):
            g0 = g * GM_CH
            w = jnp.where(is_sample, gw_ref[1, g], gw_ref[0, g])
            bias = jnp.where(is_sample, gbias_ref[1, :, g0:g0 + GM_CH], gbias_ref[0, :, g0:g0 + GM_CH])
            s = _dot(w, v[r0:r0 + CHUNK, g0:g0 + GM_CH].astype(BF16)) + bias
            go_ref[r0:r0 + CHUNK, g0:g0 + GM_CH] = (u[r0:r0 + CHUNK, g0:g0 + GM_CH] * s).astype(BF16)

    for j in range(FOX_HEADS // 2):
        pair = z(C_FQ + j * LANES, LANES) * FOX_SCALE
        fq3_ref[2 * j] = pair.astype(BF16)
        fq3_ref[2 * j + 1] = pltpu.roll(pair, LANES // 2, 1).astype(BF16)
    fk = z(C_FK, LANES)
    fv = z(C_FV, LANES)
    fk_ref[...] = fk
    fv_ref[...] = fv
    fk3_ref[0] = jnp.where(low_half, fk, 0.0).astype(BF16)
    fk3_ref[1] = jnp.where(low_half, pltpu.roll(fk, LANES // 2, 1), 0.0).astype(BF16)
    fv3_ref[0] = jnp.where(low_half, fv, 0.0).astype(BF16)
    fv3_ref[1] = jnp.where(low_half, pltpu.roll(fv, LANES // 2, 1), 0.0).astype(BF16)
    lf_ref[...] = _log_sigmoid(z(C_FL, LANES) + bf_ref[...])


def _inproj(hb, lw, cos_t, sin_t, n_prompt_chunks, tm):
    tt, d = hb.shape
    row = lambda w: pl.BlockSpec((tm, w), lambda i: (i, 0))
    out_shapes = (
        jax.ShapeDtypeStruct((tt, MLA_HEADS * Q_HEAD_COLS), BF16),
        jax.ShapeDtypeStruct((tt, KV_LORA), F32),
        jax.ShapeDtypeStruct((tt, LANES), F32),
        jax.ShapeDtypeStruct((tt, MLA_HEADS * (NOPE_DIM + MLA_V_DIM)), BF16),
        jax.ShapeDtypeStruct((tt, GM_WIDTH), F32),
        jax.ShapeDtypeStruct((tt, GM_WIDTH), BF16),
        jax.ShapeDtypeStruct((FOX_HEADS, tt, LANES), BF16),
        jax.ShapeDtypeStruct((tt, LANES), F32),
        jax.ShapeDtypeStruct((tt, LANES), F32),
        jax.ShapeDtypeStruct((FOX_KV_HEADS, tt, LANES), BF16),
        jax.ShapeDtypeStruct((FOX_KV_HEADS, tt, LANES), BF16),
        jax.ShapeDtypeStruct((tt, LANES), F32),
    )
    out_specs = (
        row(MLA_HEADS * Q_HEAD_COLS), row(KV_LORA), row(LANES), row(MLA_HEADS * (NOPE_DIM + MLA_V_DIM)),
        row(GM_WIDTH), row(GM_WIDTH),
        pl.BlockSpec((FOX_HEADS, tm, LANES), lambda i: (0, i, 0)),
        row(LANES), row(LANES),
        pl.BlockSpec((FOX_KV_HEADS, tm, LANES), lambda i: (0, i, 0)),
        pl.BlockSpec((FOX_KV_HEADS, tm, LANES), lambda i: (0, i, 0)),
        row(LANES),
    )
    in_specs = [
        row(d), _resident((d, IN_COLS_P)), _resident((1, Q_LORA)), _resident((1, KV_LORA)),
        _resident((Q_LORA, MLA_HEADS * Q_HEAD_COLS)), _resident((KV_LORA, MLA_HEADS * (NOPE_DIM + MLA_V_DIM))),
        row(LANES), row(LANES),
        _resident((1, GM_WIDTH)), _resident((1, GM_WIDTH)), _resident((1, LANES)),
        _resident((2, GM_GROUPS, CHUNK, CHUNK)), _resident((2, CHUNK, GM_WIDTH)),
    ]
    return pl.pallas_call(
        functools.partial(_inproj_kernel, n_prompt_chunks, tm),
        grid=(tt // tm,), in_specs=in_specs, out_specs=out_specs, out_shape=out_shapes,
        compiler_params=_cparams(("parallel",)), name="inproj",
    )(hb, lw["w_in"], lw["q_norm_g"], lw["kv_norm_g"], lw["w_uq"], lw["w_ukv"], cos_t, sin_t,
      lw["gm_ln_g"], lw["gm_ln_b"], lw["fox_bf"], lw["gm_w"], lw["gm_bias"])


def _flash_causal(t, blk, load_q, load_k, load_v, dv, store_o, load_cq=None, load_ck=None):
    row = lax.broadcasted_iota(I32, (blk, blk), 0)
    col = lax.broadcasted_iota(I32, (blk, blk), 1)
    causal = col <= row

    def q_body(qi, carry):
        q0 = pl.multiple_of(qi * blk, blk)
        q = load_q(q0)
        cq = load_cq(q0) if load_cq is not None else None

        def scores(j, k0):
            s = _dot_nt(q, load_k(k0))
            if cq is not None:
                s = s + cq - load_ck(j)
            return s

        def update(s, k0, m, l, acc):
            m_new = jnp.maximum(m, jnp.max(s, axis=1, keepdims=True))
            a = jnp.exp(m - m_new)
            p = jnp.exp(s - m_new)
            l_new = a * l + jnp.sum(p, axis=1, keepdims=True)
            acc_new = a * acc + _dot(p.astype(BF16), load_v(k0))
            return m_new, l_new, acc_new

        def kv_body(j, c):
            k0 = pl.multiple_of(j * blk, blk)
            return update(scores(j, k0), k0, *c)

        init = (jnp.full((blk, 1), NEG_INF, F32), jnp.zeros((blk, 1), F32), jnp.zeros((blk, dv), F32))
        m, l, acc = lax.fori_loop(0, qi, kv_body, init)
        s = jnp.where(causal, scores(qi, q0), NEG_INF)
        m, l, acc = update(s, q0, m, l, acc)
        store_o(q0, acc / l)
        return carry

    lax.fori_loop(0, t // blk, q_body, 0)


def _mla_prompt_kernel(t, blk, q_ref, kn_ref, v_ref, kr_ref, o_ref, kcat_ref):
    kcat_ref[:, 0:NOPE_DIM] = kn_ref[...]
    kcat_ref[:, NOPE_DIM:2 * NOPE_DIM] = kr_ref[...].astype(BF16)

    def store(q0, o):
        o_ref[pl.ds(q0, blk), :] = o.astype(BF16)

    _flash_causal(t, blk,
                  lambda q0: q_ref[pl.ds(q0, blk), :],
                  lambda k0: kcat_ref[pl.ds(k0, blk), :],
                  lambda k0: v_ref[pl.ds(k0, blk), :],
                  MLA_V_DIM, store)


def _mla_prompt(q, kvu, kr, n_batch, t, blk):
    tp = n_batch * t
    return pl.pallas_call(
        functools.partial(_mla_prompt_kernel, t, blk),
        grid=(n_batch, MLA_HEADS),
        in_specs=[
            pl.BlockSpec((t, Q_HEAD_COLS), lambda b, h: (b, h)),
            pl.BlockSpec((t, NOPE_DIM), lambda b, h: (b, 2 * h)),
            pl.BlockSpec((t, MLA_V_DIM), lambda b, h: (b, 2 * h + 1)),
            pl.BlockSpec((t, LANES), lambda b, h: (b, 0)),
        ],
        out_specs=pl.BlockSpec((t, MLA_V_DIM), lambda b, h: (b, h)),
        out_shape=jax.ShapeDtypeStruct((tp, MLA_HEADS * MLA_V_DIM), BF16),
        scratch_shapes=[pltpu.VMEM((t, 2 * NOPE_DIM), BF16)],
        compiler_params=_cparams(("parallel", "parallel")), name="mla_prompt",
    )(q, kvu, kvu, kr)


def _cumsum_kernel(t, lf_ref, tri_ref, c_ref):
    carry = jnp.zeros((1, LANES), F32)
    tri = tri_ref[...]
    for c in range(t // CHUNK):
        x = lf_ref[c * CHUNK:(c + 1) * CHUNK, :]
        x1 = x.astype(BF16)
        r1 = x - x1.astype(F32)
        x2 = r1.astype(BF16)
        x3 = (r1 - x2.astype(F32)).astype(BF16)
        cs = _dot(tri, x1) + _dot(tri, x2) + _dot(tri, x3) + carry
        c_ref[c * CHUNK:(c + 1) * CHUNK, :] = cs
        carry = cs[CHUNK - 1:CHUNK, :]


def _fox_cumsum(lf, tri, n_batch, t):
    return pl.pallas_call(
        functools.partial(_cumsum_kernel, t),
        grid=(n_batch,),
        in_specs=[pl.BlockSpec((t, LANES), lambda b: (b, 0)), _resident((CHUNK, CHUNK))],
        out_specs=pl.BlockSpec((t, LANES), lambda b: (b, 0)),
        out_shape=jax.ShapeDtypeStruct((n_batch * t, LANES), F32),
        compiler_params=_cparams(("parallel",)), name="fox_cumsum",
    )(lf, tri)


def _fox_prompt_kernel(t, blk, q_ref, k_ref, v_ref, cq_ref, ck_ref, o_ref):
    for i in range(2):
        def store(q0, o, i=i):
            if i == 0:
                o_ref[pl.ds(q0, blk), :] = o.astype(BF16)
            else:
                prev = o_ref[pl.ds(q0, blk), :]
                o_ref[pl.ds(q0, blk), :] = prev + pltpu.roll(o, LANES // 2, 1).astype(BF16)

        _flash_causal(t, blk,
                      lambda q0, i=i: q_ref[i, pl.ds(q0, blk), :],
                      lambda k0: k_ref[0, pl.ds(k0, blk), :],
                      lambda k0: v_ref[0, pl.ds(k0, blk), :],
                      LANES, store,
                      load_cq=lambda q0, i=i: cq_ref[0, i, pl.ds(q0, blk), :],
                      load_ck=lambda j, i=i: ck_ref[0, i, j])


def _fox_prompt(fq3, fk3, fv3, c_col, c_row, n_batch, t, blk):
    tp = n_batch * t
    nb = t // blk
    pairs_per_kv = FOX_GROUP // 2
    return pl.pallas_call(
        functools.partial(_fox_prompt_kernel, t, blk),
        grid=(n_batch, FOX_HEADS // 2),
        in_specs=[
            pl.BlockSpec((2, t, LANES), lambda b, j: (j, b, 0)),
            pl.BlockSpec((1, t, LANES), lambda b, j: (j // pairs_per_kv, b, 0)),
            pl.BlockSpec((1, t, LANES), lambda b, j: (j // pairs_per_kv, b, 0)),
            pl.BlockSpec((1, 2, t, 1), lambda b, j: (b, j, 0, 0)),
            pl.BlockSpec((1, 2, nb, 1, blk), lambda b, j: (b, j, 0, 0, 0)),
        ],
        out_specs=pl.BlockSpec((t, LANES), lambda b, j: (b, j)),
        out_shape=jax.ShapeDtypeStruct((tp, FOX_WIDTH), BF16),
        compiler_params=_cparams(("parallel", "parallel")), name="fox_prompt",
    )(fq3, fk3, fv3, c_col, c_row)


def _heads_mm_kernel(n_heads, kh, nh, x_ref, w_ref, o_ref):
    for h in range(n_heads):
        o_ref[:, h * nh:(h + 1) * nh] = _dot(x_ref[:, h * kh:(h + 1) * kh].astype(BF16), w_ref[h]).astype(o_ref.dtype)


def _heads_mm(x, w3, out_dtype, name):
    m = x.shape[0]
    n_heads, kh, nh = w3.shape
    return pl.pallas_call(
        functools.partial(_heads_mm_kernel, n_heads, kh, nh),
        in_specs=[pl.BlockSpec(memory_space=pltpu.VMEM), pl.BlockSpec(memory_space=pltpu.VMEM)],
        out_specs=pl.BlockSpec(memory_space=pltpu.VMEM),
        out_shape=jax.ShapeDtypeStruct((m, n_heads * nh), out_dtype),
        compiler_params=pltpu.CompilerParams(vmem_limit_bytes=VMEM_LIMIT), name=name,
    )(x, w3)


def _page_copies(pt_ref, bb, slot, n_pages, page_size, layer, specs, sems):
    copies = []
    for p in range(n_pages):
        page = pt_ref[bb, p]
        for k, (hbm, buf, on_rows) in enumerate(specs):
            if on_rows:
                dst = buf.at[slot, pl.ds(p * page_size, page_size), :]
            else:
                dst = buf.at[slot, :, pl.ds(p * page_size, page_size)]
            copies.append(pltpu.make_async_copy(hbm.at[layer, page], dst, sems.at[slot, k]))
    return copies


def _prefetch_pages(pt_ref, n_pages, page_size, layer, specs, sems):
    b = pl.program_id(0)
    nb = pl.num_programs(0)

    @pl.when(b == 0)
    def _():
        for c in _page_copies(pt_ref, 0, 0, n_pages, page_size, layer, specs, sems):
            c.start()

    @pl.when(b + 1 < nb)
    def _():
        for c in _page_copies(pt_ref, b + 1, (b + 1) % 2, n_pages, page_size, layer, specs, sems):
            c.start()

    slot = b % 2
    for c in _page_copies(pt_ref, b, slot, n_pages, page_size, layer, specs, sems):
        c.wait()
    return slot


def _mla_decode_kernel(layer, n_pages, page_size, pt_ref,
                       qa_ref, qr_ref, cnew_ref, krnew_ref, ckv_hbm, krt_hbm,
                       o_ref, ckv_buf, kr_buf, ckv_bf, sems):
    slot = _prefetch_pages(pt_ref, n_pages, page_size, layer,
                           [(ckv_hbm, ckv_buf, True), (krt_hbm, kr_buf, False)], sems)
    ckv_bf[...] = ckv_buf[slot].astype(BF16)
    qa = qa_ref[0]
    qr = qr_ref[0]
    s = _dot_nt(qa, ckv_bf[...]) + _dot(qr[:, 0:ROPE_DIM], kr_buf[slot].astype(BF16))
    cn = cnew_ref[0].astype(BF16).astype(F32)
    kn = krnew_ref[0].astype(BF16).astype(F32)
    s_new = (jnp.sum(qa.astype(F32) * cn, axis=1, keepdims=True)
             + jnp.sum(qr.astype(F32) * kn, axis=1, keepdims=True))
    m = jnp.maximum(jnp.max(s, axis=1, keepdims=True), s_new)
    p = jnp.exp(s - m)
    p_new = jnp.exp(s_new - m)
    l = jnp.sum(p, axis=1, keepdims=True) + p_new
    o = _dot(p.astype(BF16), ckv_bf[...]) + p_new.astype(BF16).astype(F32) * cn
    o_ref[0] = o / l


def _mla_decode(layer, page_table, qa3, qr3, cnew3, krnew3, cache_kv, cache_krt):
    db, n_pages = page_table.shape
    page_size = cache_kv.shape[2]
    past = n_pages * page_size
    per_b = lambda *tail: pl.BlockSpec((1,) + tail, lambda b, pt: (b,) + (0,) * len(tail))
    grid_spec = pltpu.PrefetchScalarGridSpec(
        num_scalar_prefetch=1, grid=(db,),
        in_specs=[per_b(MLA_HEADS, KV_LORA), per_b(MLA_HEADS, LANES), per_b(1, KV_LORA), per_b(1, LANES),
                  pl.BlockSpec(memory_space=pl.ANY), pl.BlockSpec(memory_space=pl.ANY)],
        out_specs=per_b(MLA_HEADS, KV_LORA),
        scratch_shapes=[pltpu.VMEM((2, past, KV_LORA), F32), pltpu.VMEM((2, ROPE_DIM, past), F32),
                        pltpu.VMEM((past, KV_LORA), BF16), pltpu.SemaphoreType.DMA((2, 2))],
    )
    return pl.pallas_call(
        functools.partial(_mla_decode_kernel, layer, n_pages, page_size),
        grid_spec=grid_spec,
        out_shape=jax.ShapeDtypeStruct((db, MLA_HEADS, KV_LORA), F32),
        compiler_params=_cparams(("arbitrary",)), name="mla_decode",
    )(page_table, qa3, qr3, cnew3, krnew3, cache_kv, cache_krt)


def _suffix_sum_lanes(x):
    n = x.shape[1]
    lane = lax.broadcasted_iota(I32, x.shape, 1)
    sh = 1
    while sh < n:
        shifted = pltpu.roll(x, n - sh, 1)
        x = x + jnp.where(lane < n - sh, shifted, 0.0)
        sh *= 2
    return x


def _fox_decode_kernel(layer, n_pages, page_size, pt_ref,
                       q_ref, knew_ref, vnew_ref, lfnew_ref, kt_hbm, vt_hbm, lft_hbm,
                       o_ref, k_buf, v_buf, lf_buf, sems):
    slot = _prefetch_pages(pt_ref, n_pages, page_size, layer,
                           [(kt_hbm, k_buf, False), (vt_hbm, v_buf, False), (lft_hbm, lf_buf, False)], sems)
    q = q_ref[0]
    lf = lf_buf[slot]
    bias = _suffix_sum_lanes(lf) - lf + lfnew_ref[0]
    s = _dot(q, k_buf[slot].astype(BF16)) + bias
    kn = knew_ref[0].astype(BF16).astype(F32)
    vn = vnew_ref[0].astype(BF16).astype(F32)
    s_new = jnp.sum(q.astype(F32) * kn, axis=1, keepdims=True)
    m = jnp.maximum(jnp.max(s, axis=1, keepdims=True), s_new)
    p = jnp.exp(s - m)
    p_new = jnp.exp(s_new - m)
    l = jnp.sum(p, axis=1, keepdims=True) + p_new
    o = _dot_nt(p.astype(BF16), v_buf[slot].astype(BF16)) + p_new.astype(BF16).astype(F32) * vn
    o_ref[0] = o / l


def _fox_decode(layer, page_table, qbd3, knew3, vnew3, lfnew3, cache_kt, cache_vt, cache_lft):
    db, n_pages = page_table.shape
    page_size = cache_kt.shape[3]
    past = n_pages * page_size
    per_b = lambda *tail: pl.BlockSpec((1,) + tail, lambda b, pt: (b,) + (0,) * len(tail))
    grid_spec = pltpu.PrefetchScalarGridSpec(
        num_scalar_prefetch=1, grid=(db,),
        in_specs=[per_b(FOX_HEADS, LANES), per_b(1, LANES), per_b(1, LANES), per_b(FOX_HEADS, 1),
                  pl.BlockSpec(memory_space=pl.ANY), pl.BlockSpec(memory_space=pl.ANY),
                  pl.BlockSpec(memory_space=pl.ANY)],
        out_specs=per_b(FOX_HEADS, LANES),
        scratch_shapes=[pltpu.VMEM((2, FOX_KV_WIDTH, past), F32), pltpu.VMEM((2, FOX_KV_WIDTH, past), F32),
                        pltpu.VMEM((2, FOX_HEADS, past), F32), pltpu.SemaphoreType.DMA((2, 3))],
    )
    return pl.pallas_call(
        functools.partial(_fox_decode_kernel, layer, n_pages, page_size),
        grid_spec=grid_spec,
        out_shape=jax.ShapeDtypeStruct((db, FOX_HEADS, LANES), F32),
        compiler_params=_cparams(("arbitrary",)), name="fox_decode",
    )(page_table, qbd3, knew3, vnew3, lfnew3, cache_kt, cache_vt, cache_lft)


def _group_top2(sel, sc):
    idx = lax.broadcasted_iota(I32, sel.shape, 0).astype(F32)
    big = float(EXPERTS_PER_GROUP)
    m1 = jnp.max(sel, axis=0, keepdims=True)
    i1 = jnp.min(jnp.where(sel == m1, idx, big), axis=0, keepdims=True)
    rest = jnp.where(idx == i1, -jnp.inf, sel)
    m2 = jnp.max(rest, axis=0, keepdims=True)
    i2 = jnp.min(jnp.where(rest == m2, idx, big), axis=0, keepdims=True)
    s1 = jnp.sum(jnp.where(idx == i1, sc, 0.0), axis=0, keepdims=True)
    s2 = jnp.sum(jnp.where(idx == i2, sc, 0.0), axis=0, keepdims=True)
    return m1 + m2, i1, i2, s1, s2


def _router_logits(rwt, x1):
    x_hi = x1.astype(BF16)
    x_lo = (x1 - x_hi.astype(F32)).astype(BF16)
    w_hi = rwt.astype(BF16)
    w_lo = (rwt - w_hi.astype(F32)).astype(BF16)
    return _dot_nt(w_hi, x_hi) + _dot_nt(w_hi, x_lo) + _dot_nt(w_lo, x_hi)


def _outproj_kernel(alpha, tm, x_ref, om_ref, og_ref, of_ref, wo_ref, g_ref, b_ref, rwt_ref, rb_ref, tri_ref,
                    x1_ref, route_ref, cnt_ref, carry_ref):
    k1 = om_ref.shape[1]
    k2 = k1 + og_ref.shape[1]
    acc = (_dot(om_ref[...], wo_ref[0:k1, :]) + _dot(og_ref[...], wo_ref[k1:k2, :])
           + _dot(of_ref[...], wo_ref[k2:, :]))
    x1 = _layer_norm(alpha * x_ref[...] + acc, g_ref[...], b_ref[...])
    x1_ref[...] = x1

    sc = _sigmoid(_router_logits(rwt_ref[...], x1))
    sel = sc + rb_ref[...]
    best = None
    for g in range(N_GROUPS):
        r0 = g * EXPERTS_PER_GROUP
        gs, i1, i2, s1, s2 = _group_top2(sel[r0:r0 + EXPERTS_PER_GROUP], sc[r0:r0 + EXPERTS_PER_GROUP])
        cand = (gs, i1 + float(r0), i2 + float(r0), s1, s2)
        if best is None:
            best = cand
        else:
            better = gs > best[0]
            best = tuple(jnp.where(better, c, o) for c, o in zip(cand, best))
    _, e1, e2, s1, s2 = best
    denom = s1 + s2
    eidx = lax.broadcasted_iota(I32, (N_EXPERTS, tm), 0).astype(F32)
    hit1 = eidx == e1
    hit2 = eidx == e2
    onehot = jnp.where(hit1 | hit2, 1.0, 0.0)

    @pl.when(pl.program_id(0) == 0)
    def _():
        carry_ref[...] = jnp.zeros_like(carry_ref)

    before = _dot(onehot.astype(BF16), tri_ref[...]) + carry_ref[:, 0:1]
    r1 = jnp.sum(jnp.where(hit1, before, 0.0), axis=0, keepdims=True)
    r2 = jnp.sum(jnp.where(hit2, before, 0.0), axis=0, keepdims=True)
    carry_ref[...] = carry_ref[...] + jnp.sum(onehot, axis=1, keepdims=True)
    cnt_ref[...] = carry_ref[...]
    zero = jnp.zeros_like(e1)
    route_ref[...] = jnp.concatenate([e1, e2, s1 / denom, s2 / denom, r1, r2, zero, zero], axis=0)


def _outproj(alpha, h, o_mla, o_gm, o_fox, lw, rwt, rb, tri, tm):
    tt, d = h.shape
    row = lambda w, dt=None: pl.BlockSpec((tm, w), lambda i: (i, 0))
    return pl.pallas_call(
        functools.partial(_outproj_kernel, alpha, tm),
        grid=(tt // tm,),
        in_specs=[row(d), row(o_mla.shape[1]), row(o_gm.shape[1]), row(o_fox.shape[1]),
                  _resident(lw["w_out"].shape), _resident((1, d)), _resident((1, d)),
                  _resident((N_EXPERTS, d)), _resident((N_EXPERTS, 1)), _resident((tm, tm))],
        out_specs=(row(d), pl.BlockSpec((SUBLANES, tm), lambda i: (0, i)),
                   pl.BlockSpec((N_EXPERTS, LANES), lambda i: (0, 0))),
        out_shape=(jax.ShapeDtypeStruct((tt, d), F32), jax.ShapeDtypeStruct((SUBLANES, tt), F32),
                   jax.ShapeDtypeStruct((N_EXPERTS, LANES), F32)),
        scratch_shapes=[pltpu.VMEM((N_EXPERTS, LANES), F32)],
        compiler_params=_cparams(("arbitrary",)), name="outproj_router",
    )(h, o_mla, o_gm, o_fox, lw["w_out"], lw["ln1_g"], lw["ln1_b"], rwt, rb, tri)


def _rowcopy_kernel(n, window, src_ref, dst_ref, in_hbm, init_hbm, out_hbm, sem):
    del init_hbm

    def row_copy(s, d):
        return pltpu.make_async_copy(in_hbm.at[pl.ds(s, 1), :], out_hbm.at[pl.ds(d, 1), :], sem)

    def body(i, carry):
        row_copy(src_ref[i], dst_ref[i]).start()

        @pl.when(i >= window)
        def _():
            row_copy(0, 0).wait()

        return carry

    lax.fori_loop(0, n, body, 0)

    def drain(i, carry):
        row_copy(0, 0).wait()
        return carry

    lax.fori_loop(0, min(n, window), drain, 0)


def _rowcopy(src_idx, dst_idx, x, out_init):
    n = src_idx.shape[0]
    grid_spec = pltpu.PrefetchScalarGridSpec(
        num_scalar_prefetch=2, grid=(1,),
        in_specs=[pl.BlockSpec(memory_space=pl.ANY), pl.BlockSpec(memory_space=pl.ANY)],
        out_specs=pl.BlockSpec(memory_space=pl.ANY),
        scratch_shapes=[pltpu.SemaphoreType.DMA(())],
    )
    return pl.pallas_call(
        functools.partial(_rowcopy_kernel, n, DMA_WINDOW),
        grid_spec=grid_spec,
        out_shape=jax.ShapeDtypeStruct(out_init.shape, out_init.dtype),
        input_output_aliases={3: 0},
        compiler_params=pltpu.CompilerParams(dimension_semantics=("arbitrary",), has_side_effects=True),
        name="rowcopy",
    )(src_idx, dst_idx, x, out_init)


def _moe_kernel(te_ref, first_ref, nused_ref, x_ref, wg_ref, wu_ref, wd_ref, y_ref, wg_bf, wu_bf, wd_bf):
    i = pl.program_id(0)

    @pl.when(first_ref[i] == 1)
    def _():
        wg_bf[...] = wg_ref[...].astype(BF16)
        wu_bf[...] = wu_ref[...].astype(BF16)
        wd_bf[...] = wd_ref[...].astype(BF16)

    @pl.when(i < nused_ref[0])
    def _():
        x = x_ref[...].astype(BF16)
        gate = _dot(x, wg_bf[...])
        a = gate * _sigmoid(gate) * _dot(x, wu_bf[...])
        y_ref[...] = _dot(a.astype(BF16), wd_bf[...])


def _moe_gemm(layer, te, first, nused, xs, w_gate, w_up, w_down, tile):
    p, d = xs.shape
    f = w_gate.shape[-1]
    nt = p // tile
    last = lambda i, nu: jnp.minimum(i, nu[0] - 1)
    grid_spec = pltpu.PrefetchScalarGridSpec(
        num_scalar_prefetch=3, grid=(nt,),
        in_specs=[
            pl.BlockSpec((tile, d), lambda i, te, fi, nu: (last(i, nu), 0)),
            pl.BlockSpec((None, None, d, f), lambda i, te, fi, nu: (layer, te[i], 0, 0)),
            pl.BlockSpec((None, None, d, f), lambda i, te, fi, nu: (layer, te[i], 0, 0)),
            pl.BlockSpec((None, None, f, d), lambda i, te, fi, nu: (layer, te[i], 0, 0)),
        ],
        out_specs=pl.BlockSpec((tile, d), lambda i, te, fi, nu: (last(i, nu), 0)),
        scratch_shapes=[pltpu.VMEM((d, f), BF16), pltpu.VMEM((d, f), BF16), pltpu.VMEM((f, d), BF16)],
    )
    return pl.pallas_call(
        _moe_kernel, grid_spec=grid_spec,
        out_shape=jax.ShapeDtypeStruct((p, d), F32),
        compiler_params=_cparams(("arbitrary",)), name="moe_gemm",
    )(te, first, nused, xs, w_gate, w_up, w_down)


def _moe(layer, x1, route, counts, w_gate, w_up, w_down):
    tt, d = x1.shape
    tile = MOE_TILE
    nt = -(-2 * tt // tile) + N_EXPERTS
    p = nt * tile
    e1 = route[0].astype(I32)
    e2 = route[1].astype(I32)
    cnt = counts[:, 0].astype(I32)
    padded = ((cnt + tile - 1) // tile) * tile
    ends = jnp.cumsum(padded)
    starts = ends - padded
    pos1 = starts[e1] + route[4].astype(I32)
    pos2 = starts[e2] + route[5].astype(I32)
    nused = (ends[-1] // tile).astype(I32)
    tile_id = jnp.minimum(jnp.arange(nt, dtype=I32), nused - 1)
    te = jnp.minimum(jnp.searchsorted(ends, tile_id * tile, side="right"), N_EXPERTS - 1).astype(I32)
    first = jnp.concatenate([jnp.ones((1,), I32), (te[1:] != te[:-1]).astype(I32)])
    tok = jnp.arange(tt, dtype=I32)
    src = jnp.concatenate([tok, tok])
    dst = jnp.concatenate([pos1, pos2])
    xs = _rowcopy(src, dst, x1, jnp.zeros((p, d), F32))
    ys = _moe_gemm(layer, te, first, nused.reshape(1), xs, w_gate, w_up, w_down, tile)
    back = _rowcopy(dst, jnp.arange(2 * tt, dtype=I32), ys, jnp.zeros((2 * tt, d), F32))
    return back.reshape(2, tt, d)


def _finish_kernel(alpha, x1_ref, y_ref, gw_ref, p_ref, wpe_ref, wpg_ref, g2_ref, b2_ref, g3_ref, b3_ref,
                   h_ref, hb_ref):
    gw = gw_ref[...]
    moe = y_ref[0] * gw[:, 2:3] + y_ref[1] * gw[:, 3:4]
    x2 = _layer_norm(alpha * x1_ref[...] + moe, g2_ref[...], b2_ref[...])
    e = _dot(p_ref[...].astype(BF16), wpe_ref[...]) * _sigmoid(_dot(x2.astype(BF16), wpg_ref[...]))
    x3 = _layer_norm(alpha * x2 + e, g3_ref[...], b3_ref[...])
    h_ref[...] = x3
    hb_ref[...] = x3.astype(BF16)


def _finish(alpha, x1, y2, route_t, p_l, lw, tm):
    tt, d = x1.shape
    row = lambda w: pl.BlockSpec((tm, w), lambda i: (i, 0))
    return pl.pallas_call(
        functools.partial(_finish_kernel, alpha),
        grid=(tt // tm,),
        in_specs=[row(d), pl.BlockSpec((2, tm, d), lambda i: (0, i, 0)), row(SUBLANES), row(p_l.shape[1]),
                  _resident(lw["w_pe"].shape), _resident(lw["w_pg"].shape),
                  _resident((1, d)), _resident((1, d)), _resident((1, d)), _resident((1, d))],
        out_specs=(row(d), row(d)),
        out_shape=(jax.ShapeDtypeStruct((tt, d), F32), jax.ShapeDtypeStruct((tt, d), BF16)),
        compiler_params=_cparams(("parallel",)), name="finish",
    )(x1, y2, route_t, p_l, lw["w_pe"], lw["w_pg"], lw["ln2_g"], lw["ln2_b"], lw["ln3_g"], lw["ln3_b"])


def _pack_layer_weights(l, w_in, q_norm_g, kv_norm_g, w_uq, w_ukv, gmlp_ln_g, gmlp_ln_b, gmlp_ws, gmlp_bs, fox_bf,
                        w_out, ln1_g, ln1_b, ln2_g, ln2_b, ln3_g, ln3_b, w_pe, w_pg):
    d = w_in.shape[1]
    wi = w_in[l]
    splits = np.cumsum([Q_LORA, KV_LORA, ROPE_DIM, GM_WIDTH, GM_WIDTH, FOX_WIDTH, FOX_KV_WIDTH, FOX_KV_WIDTH])
    q_lat, kv_lat, k_r, u, v, fq, fk, fv, fl = jnp.split(wi, [int(s) for s in splits], axis=1)
    w_in_p = jnp.concatenate(
        [q_lat, kv_lat, k_r, k_r, u, v, fq, fk, fv, fl, jnp.zeros((d, LANES - FOX_HEADS), F32)], axis=1)
    uq = w_uq[l].reshape(Q_LORA, MLA_HEADS, NOPE_DIM + ROPE_DIM)
    w_uq_p = jnp.concatenate([uq, uq[:, :, NOPE_DIM:]], axis=2).reshape(Q_LORA, MLA_HEADS * Q_HEAD_COLS)
    ukv = w_ukv[l].reshape(KV_LORA, MLA_HEADS, NOPE_DIM + MLA_V_DIM)
    w_uk_t = jnp.transpose(ukv[:, :, :NOPE_DIM], (1, 2, 0))
    w_uv = jnp.transpose(ukv[:, :, NOPE_DIM:], (1, 0, 2))
    ws = gmlp_ws[l]
    tril = jnp.tril(jnp.ones((CHUNK, CHUNK), dtype=bool))
    gm_w = jnp.stack([jnp.where(tril, ws, 0.0),
                      ws[:, 0:1, 0:1] * jnp.eye(CHUNK, dtype=F32)[None]])
    bs = gmlp_bs[l]
    bias_t = jnp.repeat(bs.T, GM_CH, axis=1)
    bias_s = jnp.broadcast_to(jnp.repeat(bs[:, 0], GM_CH)[None, :], (CHUNK, GM_WIDTH))
    bf = jnp.concatenate([fox_bf[l], jnp.zeros((LANES - FOX_HEADS,), F32)])[None, :]
    r = lambda a: a[l][None, :]
    return dict(
        w_in=w_in_p.astype(BF16), q_norm_g=r(q_norm_g), kv_norm_g=r(kv_norm_g), w_uq=w_uq_p.astype(BF16),
        w_ukv=w_ukv[l].astype(BF16), w_uk_t=w_uk_t.astype(BF16), w_uv=w_uv.astype(BF16),
        gm_ln_g=r(gmlp_ln_g), gm_ln_b=r(gmlp_ln_b), gm_w=gm_w.astype(BF16), gm_bias=jnp.stack([bias_t, bias_s]),
        fox_bf=bf, w_out=w_out[l].astype(BF16), ln1_g=r(ln1_g), ln1_b=r(ln1_b), ln2_g=r(ln2_g), ln2_b=r(ln2_b),
        ln3_g=r(ln3_g), ln3_b=r(ln3_b), w_pe=w_pe[l].astype(BF16), w_pg=w_pg[l].astype(BF16))


def _row_tile(tt):
    for tm in range(ROW_TILE_CAP, 0, -CHUNK):
        if tt % tm == 0:
            return tm
    raise ValueError("stacked row count must be a multiple of the gMLP chunk")


def kernel(x_prompt, x_sample, cache_mla_kv, cache_mla_krope, cache_fox_k, cache_fox_v, cache_fox_logf, page_table, p_prompt, p_sample, w_in, q_norm_g, kv_norm_g, w_uq, w_ukv, gmlp_ln_g, gmlp_ln_b, gmlp_ws, gmlp_bs, fox_bf, w_out, ln1_g, ln1_b, ln2_g, ln2_b, ln3_g, ln3_b, router_w, router_b, w_e_gate, w_e_up, w_e_down, w_pe, w_pg):
    n_batch, t, d = x_prompt.shape
    db, dec_seq, _ = x_sample.shape
    depth = w_in.shape[0]
    n_pages = page_table.shape[1]
    page_size = cache_mla_kv.shape[2]
    past = n_pages * page_size
    assert dec_seq == 1 and t % CHUNK == 0 and db % CHUNK == 0
    tp = n_batch * t
    tt = tp + db
    tm = _row_tile(tt)
    tm_fin = tm // 2 if (tm // 2) % SUBLANES == 0 else tm
    blk = min(ATT_BLOCK, t)
    nb = t // blk
    alpha = (2 * depth) ** 0.25

    pos = jnp.concatenate([jnp.tile(jnp.arange(t), n_batch), jnp.full((db,), past)]).astype(F32)
    half = ROPE_DIM // 2
    inv = ROPE_BASE ** (-jnp.arange(half, dtype=F32) / half)
    ang = pos[:, None] * inv[None, :]
    cos_t = jnp.tile(jnp.cos(ang), (1, LANES // half))
    sin_h = jnp.sin(ang)
    sin_t = jnp.tile(jnp.concatenate([-sin_h, sin_h], axis=1), (1, LANES // ROPE_DIM))

    cache_krt = jnp.transpose(cache_mla_krope, (0, 1, 3, 2))
    n_pool = cache_fox_k.shape[1]
    cache_kt = jnp.transpose(cache_fox_k, (0, 1, 3, 4, 2)).reshape(depth, n_pool, FOX_KV_WIDTH, page_size)
    cache_vt = jnp.transpose(cache_fox_v, (0, 1, 3, 4, 2)).reshape(depth, n_pool, FOX_KV_WIDTH, page_size)
    cache_lft = jnp.transpose(cache_fox_logf, (0, 1, 3, 2))

    tri_lower = jnp.tril(jnp.ones((CHUNK, CHUNK), F32)).astype(BF16)
    tri_before = jnp.triu(jnp.ones((tm, tm), F32), k=1).astype(BF16)
    rwt = router_w.T
    rb = router_b[:, None]

    h = jnp.concatenate([x_prompt.reshape(tp, d), x_sample.reshape(db, d)], axis=0)
    hb = h.astype(BF16)
    outs = [[] for _ in range(11)]
    for l in range(depth):
        lw = _pack_layer_weights(l, w_in, q_norm_g, kv_norm_g, w_uq, w_ukv, gmlp_ln_g, gmlp_ln_b, gmlp_ws, gmlp_bs,
                                 fox_bf, w_out, ln1_g, ln1_b, ln2_g, ln2_b, ln3_g, ln3_b, w_pe, w_pg)
        q, ckv, kr, kvu, gv, go, fq3, fk, fv, fk3, fv3, lf = _inproj(hb, lw, cos_t, sin_t, tp // CHUNK, tm)

        o_mla_p = _mla_prompt(q, kvu, kr, n_batch, t, blk)
        c = _fox_cumsum(lf, tri_lower, n_batch, t)[:, :FOX_HEADS].reshape(n_batch, t, FOX_HEADS)
        c_hm = jnp.transpose(c, (0, 2, 1))
        o_fox_p = _fox_prompt(fq3, fk3, fv3, c_hm[..., None], c_hm.reshape(n_batch, FOX_HEADS, nb, 1, blk),
                              n_batch, t, blk)

        q_s = q[tp:].reshape(db, MLA_HEADS, Q_HEAD_COLS)
        qa = _heads_mm(q_s[:, :, :NOPE_DIM].reshape(db, MLA_HEADS * NOPE_DIM), lw["w_uk_t"], BF16, "mla_absorb_q")
        o_lat = _mla_decode(l, page_table, qa.reshape(db, MLA_HEADS, KV_LORA), q_s[:, :, NOPE_DIM:],
                            ckv[tp:, None, :], kr[tp:, None, :], cache_mla_kv, cache_krt)
        o_mla_s = _heads_mm(o_lat.reshape(db, MLA_HEADS * KV_LORA), lw["w_uv"], BF16, "mla_absorb_v")
        fq_s = fq3[:, tp:, :FOX_HEAD_DIM]
        zeros_q = jnp.zeros_like(fq_s)
        kvh = (jnp.arange(FOX_HEADS) // FOX_GROUP)[:, None, None]
        qbd = jnp.concatenate([jnp.where(kvh == 0, fq_s, zeros_q), jnp.where(kvh == 1, fq_s, zeros_q)], axis=2)
        o8 = _fox_decode(l, page_table, jnp.transpose(qbd, (1, 0, 2)), fk[tp:, None, :], fv[tp:, None, :],
                         lf[tp:, :FOX_HEADS, None], cache_kt, cache_vt, cache_lft)
        o8 = o8.reshape(db, FOX_KV_HEADS, FOX_GROUP, FOX_KV_HEADS, FOX_HEAD_DIM)
        o_fox_s = jnp.stack([o8[:, n, :, n, :] for n in range(FOX_KV_HEADS)], axis=1).reshape(db, FOX_WIDTH)

        o_mla = jnp.concatenate([o_mla_p, o_mla_s], axis=0)
        o_fox = jnp.concatenate([o_fox_p, o_fox_s.astype(BF16)], axis=0)
        x1, route, counts = _outproj(alpha, h, o_mla, go, o_fox, lw, rwt, rb, tri_before, tm)
        y2 = _moe(l, x1, route, counts, w_e_gate, w_e_up, w_e_down)
        p_l = jnp.concatenate([p_prompt[l].reshape(tp, -1), p_sample[l].reshape(db, -1)], axis=0)
        h, hb = _finish(alpha, x1, y2, route.T, p_l, lw, tm_fin)

        leaves = (ckv[:tp].reshape(n_batch, t, KV_LORA), kr[:tp, :ROPE_DIM].reshape(n_batch, t, ROPE_DIM),
                  fk[:tp].reshape(n_batch, t, FOX_KV_HEADS, FOX_HEAD_DIM),
                  fv[:tp].reshape(n_batch, t, FOX_KV_HEADS, FOX_HEAD_DIM),
                  lf[:tp, :FOX_HEADS].reshape(n_batch, t, FOX_HEADS),
                  ckv[tp:].reshape(db, 1, KV_LORA), kr[tp:, :ROPE_DIM].reshape(db, 1, ROPE_DIM),
                  fk[tp:].reshape(db, 1, FOX_KV_HEADS, FOX_HEAD_DIM), fv[tp:].reshape(db, 1, FOX_KV_HEADS, FOX_HEAD_DIM),
                  lf[tp:, :FOX_HEADS].reshape(db, 1, FOX_HEADS), gv[tp:].reshape(db, 1, GM_WIDTH))
        for acc, leaf in zip(outs, leaves):
            acc.append(leaf)
    return (h[:tp].reshape(n_batch, t, d), h[tp:].reshape(db, 1, d)) + tuple(jnp.stack(o) for o in outs)
```

```python
import functools

import jax
import jax.numpy as jnp
import numpy as np
from jax import lax
from jax.experimental import pallas as pl
from jax.experimental.pallas import tpu as pltpu

F32 = jnp.float32
BF16 = jnp.bfloat16
I32 = jnp.int32

MLA_HEADS = 8
Q_LORA = 512
KV_LORA = 256
NOPE_DIM = 128
ROPE_DIM = 64
MLA_V_DIM = 128
ROPE_BASE = 10000.0
MLA_SCALE = (NOPE_DIM + ROPE_DIM) ** -0.5
GM_GROUPS = 4
GM_CH = 128
GM_WIDTH = GM_GROUPS * GM_CH
CHUNK = 128
FOX_HEADS = 8
FOX_KV_HEADS = 2
FOX_GROUP = FOX_HEADS // FOX_KV_HEADS
FOX_HEAD_DIM = 64
FOX_WIDTH = FOX_HEADS * FOX_HEAD_DIM
FOX_KV_WIDTH = FOX_KV_HEADS * FOX_HEAD_DIM
FOX_SCALE = FOX_HEAD_DIM ** -0.5
N_EXPERTS = 32
N_GROUPS = 4
EXPERTS_PER_GROUP = N_EXPERTS // N_GROUPS
D_EXPERT = 512
LN_EPS = 1e-5
RMS_EPS = 1e-6
NEG_INF = -1e30

LANES = 128
SUBLANES = 8
VMEM_LIMIT = 56 * 1024 * 1024

ROW_TILE_CAP = 640
ATT_BLOCK = 256
MOE_TILE = 256
DMA_WINDOW = 128

C_QLAT = 0
C_KVLAT = C_QLAT + Q_LORA
C_KR = C_KVLAT + KV_LORA
C_U = C_KR + LANES
C_V = C_U + GM_WIDTH
C_FQ = C_V + GM_WIDTH
C_FK = C_FQ + FOX_WIDTH
C_FV = C_FK + FOX_KV_WIDTH
C_FL = C_FV + FOX_KV_WIDTH
IN_COLS_P = C_FL + LANES
Q_HEAD_COLS = 2 * LANES


def _cparams(sem, **kw):
    return pltpu.CompilerParams(dimension_semantics=sem, vmem_limit_bytes=VMEM_LIMIT, **kw)


def _dot(a, b):
    return jnp.dot(a, b, preferred_element_type=F32)


def _dot_nt(a, b):
    return lax.dot_general(a, b, (((1,), (1,)), ((), ())), preferred_element_type=F32)


def _layer_norm(x, g, b):
    mu = jnp.mean(x, axis=-1, keepdims=True)
    xc = x - mu
    var = jnp.mean(xc * xc, axis=-1, keepdims=True)
    return xc * lax.rsqrt(var + LN_EPS) * g + b


def _rms_norm(x, g):
    return x * lax.rsqrt(jnp.mean(x * x, axis=-1, keepdims=True) + RMS_EPS) * g


def _gelu_tanh(x):
    return 0.5 * x * (1.0 + jnp.tanh(0.7978845608028654 * (x + 0.044715 * (x * x * x))))


def _log_sigmoid(x):
    return jnp.minimum(x, 0.0) - jnp.log1p(jnp.exp(-jnp.abs(x)))


def _sigmoid(x):
    return 1.0 / (1.0 + jnp.exp(-x))


def _resident(shape):
    nd = len(shape)
    return pl.BlockSpec(shape, lambda *_: (0,) * nd, pipeline_mode=pl.Buffered(1))


def _inproj_kernel(n_prompt_chunks, tm,
                   hb_ref, win_ref, qg_ref, kvg_ref, wuq_ref, wukv_ref, cos_ref, sin_ref,
                   gg_ref, gb_ref, bf_ref, gw_ref, gbias_ref,
                   q_ref, ckv_ref, kr_ref, kvu_ref, gv_ref, go_ref, fq3_ref, fk_ref, fv_ref,
                   fk3_ref, fv3_ref, lf_ref):
    hb = hb_ref[...]

    def z(c0, width):
        return _dot(hb, win_ref[:, c0:c0 + width])

    cos = cos_ref[...]
    sin = sin_ref[...]
    lane = lax.broadcasted_iota(I32, (tm, LANES), 1)
    low_half = lane < (LANES // 2)

    def rope(blk):
        return blk * cos + pltpu.roll(blk, ROPE_DIM // 2, 1) * sin

    qn = _rms_norm(z(C_QLAT, Q_LORA), qg_ref[...])
    qf = _dot(qn.astype(BF16), wuq_ref[...])
    for h in range(MLA_HEADS):
        c0 = h * Q_HEAD_COLS
        q_ref[:, c0:c0 + LANES] = (qf[:, c0:c0 + LANES] * MLA_SCALE).astype(BF16)
        q_ref[:, c0 + LANES:c0 + 2 * LANES] = (rope(qf[:, c0 + LANES:c0 + 2 * LANES]) * MLA_SCALE).astype(BF16)

    ckv = _rms_norm(z(C_KVLAT, KV_LORA), kvg_ref[...])
    ckv_ref[...] = ckv
    kvu_ref[...] = _dot(ckv.astype(BF16), wukv_ref[...]).astype(BF16)
    kr_ref[...] = jnp.where(low_half, rope(z(C_KR, LANES)), 0.0)

    u = _gelu_tanh(z(C_U, GM_WIDTH))
    v = _layer_norm(_gelu_tanh(z(C_V, GM_WIDTH)), gg_ref[...], gb_ref[...])
    gv_ref[...] = v
    nch = tm // CHUNK
    for c in range(nch):
        is_sample = (pl.program_id(0) * nch + c) >= n_prompt_chunks
        r0 = c * CHUNK
        for g in range(GM_GROUPS):
            g0 = g * GM_CH
            w = jnp.where(is_sample, gw_ref[1, g], gw_ref[0, g])
            bias = jnp.where(is_sample, gbias_ref[1, :, g0:g0 + GM_CH], gbias_ref[0, :, g0:g0 + GM_CH])
            s = _dot(w, v[r0:r0 + CHUNK, g0:g0 + GM_CH].astype(BF16)) + bias
            go_ref[r0:r0 + CHUNK, g0:g0 + GM_CH] = (u[r0:r0 + CHUNK, g0:g0 + GM_CH] * s).astype(BF16)

    for j in range(FOX_HEADS // 2):
        pair = z(C_FQ + j * LANES, LANES) * FOX_SCALE
        fq3_ref[2 * j] = pair.astype(BF16)
        fq3_ref[2 * j + 1] = pltpu.roll(pair, LANES // 2, 1).astype(BF16)
    fk = z(C_FK, LANES)
    fv = z(C_FV, LANES)
    fk_ref[...] = fk
    fv_ref[...] = fv
    fk3_ref[0] = jnp.where(low_half, fk, 0.0).astype(BF16)
    fk3_ref[1] = jnp.where(low_half, pltpu.roll(fk, LANES // 2, 1), 0.0).astype(BF16)
    fv3_ref[0] = jnp.where(low_half, fv, 0.0).astype(BF16)
    fv3_ref[1] = jnp.where(low_half, pltpu.roll(fv, LANES // 2, 1), 0.0).astype(BF16)
    lf_ref[...] = _log_sigmoid(z(C_FL, LANES) + bf_ref[...])


def _inproj(hb, lw, cos_t, sin_t, n_prompt_chunks, tm):
    tt, d = hb.shape
    row = lambda w: pl.BlockSpec((tm, w), lambda i: (i, 0))
    out_shapes = (
        jax.ShapeDtypeStruct((tt, MLA_HEADS * Q_HEAD_COLS), BF16),
        jax.ShapeDtypeStruct((tt, KV_LORA), F32),
        jax.ShapeDtypeStruct((tt, LANES), F32),
        jax.ShapeDtypeStruct((tt, MLA_HEADS * (NOPE_DIM + MLA_V_DIM)), BF16),
        jax.ShapeDtypeStruct((tt, GM_WIDTH), F32),
        jax.ShapeDtypeStruct((tt, GM_WIDTH), BF16),
        jax.ShapeDtypeStruct((FOX_HEADS, tt, LANES), BF16),
        jax.ShapeDtypeStruct((tt, LANES), F32),
        jax.ShapeDtypeStruct((tt, LANES), F32),
        jax.ShapeDtypeStruct((FOX_KV_HEADS, tt, LANES), BF16),
        jax.ShapeDtypeStruct((FOX_KV_HEADS, tt, LANES), BF16),
        jax.ShapeDtypeStruct((tt, LANES), F32),
    )
    out_specs = (
        row(MLA_HEADS * Q_HEAD_COLS), row(KV_LORA), row(LANES), row(MLA_HEADS * (NOPE_DIM + MLA_V_DIM)),
        row(GM_WIDTH), row(GM_WIDTH),
        pl.BlockSpec((FOX_HEADS, tm, LANES), lambda i: (0, i, 0)),
        row(LANES), row(LANES),
        pl.BlockSpec((FOX_KV_HEADS, tm, LANES), lambda i: (0, i, 0)),
        pl.BlockSpec((FOX_KV_HEADS, tm, LANES), lambda i: (0, i, 0)),
        row(LANES),
    )
    in_specs = [
        row(d), _resident((d, IN_COLS_P)), _resident((1, Q_LORA)), _resident((1, KV_LORA)),
        _resident((Q_LORA, MLA_HEADS * Q_HEAD_COLS)), _resident((KV_LORA, MLA_HEADS * (NOPE_DIM + MLA_V_DIM))),
        row(LANES), row(LANES),
        _resident((1, GM_WIDTH)), _resident((1, GM_WIDTH)), _resident((1, LANES)),
        _resident((2, GM_GROUPS, CHUNK, CHUNK)), _resident((2, CHUNK, GM_WIDTH)),
    ]
    return pl.pallas_call(
        functools.partial(_inproj_kernel, n_prompt_chunks, tm),
        grid=(tt // tm,), in_specs=in_specs, out_specs=out_specs, out_shape=out_shapes,
        compiler_params=_cparams(("parallel",)), name="inproj",
    )(hb, lw["w_in"], lw["q_norm_g"], lw["kv_norm_g"], lw["w_uq"], lw["w_ukv"], cos_t, sin_t,
      lw["gm_ln_g"], lw["gm_ln_b"], lw["fox_bf"], lw["gm_w"], lw["gm_bias"])


def _attend_causal(t, blk, load_q, k_ref, v_ref, store_o, cq_ref=None, ck_ref=None):
    row = lax.broadcasted_iota(I32, (blk, blk), 0)
    col = lax.broadcasted_iota(I32, (blk, blk), 1)
    causal = col <= row
    for qi in range(t // blk):
        q0 = qi * blk
        e = q0 + blk
        s = _dot_nt(load_q(q0), k_ref[0:e, :])
        if cq_ref is not None:
            s = s + cq_ref[q0:e, :] - ck_ref[:, 0:e]
        s_d = jnp.where(causal, s[:, q0:e], NEG_INF)
        m = jnp.max(s_d, axis=1, keepdims=True)
        if qi > 0:
            s_o = s[:, 0:q0]
            m = jnp.maximum(m, jnp.max(s_o, axis=1, keepdims=True))
            p_o = jnp.exp(s_o - m)
        p_d = jnp.exp(s_d - m)
        l = jnp.sum(p_d, axis=1, keepdims=True)
        o = _dot(p_d.astype(BF16), v_ref[q0:e, :])
        if qi > 0:
            l = l + jnp.sum(p_o, axis=1, keepdims=True)
            o = o + _dot(p_o.astype(BF16), v_ref[0:q0, :])
        store_o(q0, o / l)


def _mla_prompt_kernel(t, blk, q_ref, kn_ref, v_ref, kr_ref, o_ref, kcat_ref):
    kcat_ref[:, 0:NOPE_DIM] = kn_ref[...]
    kcat_ref[:, NOPE_DIM:2 * NOPE_DIM] = kr_ref[...].astype(BF16)

    def store(q0, o):
        o_ref[q0:q0 + blk, :] = o.astype(BF16)

    _attend_causal(t, blk, lambda q0: q_ref[q0:q0 + blk, :], kcat_ref, v_ref, store)


def _mla_prompt(q, kvu, kr, n_batch, t, blk):
    tp = n_batch * t
    return pl.pallas_call(
        functools.partial(_mla_prompt_kernel, t, blk),
        grid=(n_batch, MLA_HEADS),
        in_specs=[
            pl.BlockSpec((t, Q_HEAD_COLS), lambda b, h: (b, h)),
            pl.BlockSpec((t, NOPE_DIM), lambda b, h: (b, 2 * h)),
            pl.BlockSpec((t, MLA_V_DIM), lambda b, h: (b, 2 * h + 1)),
            pl.BlockSpec((t, LANES), lambda b, h: (b, 0)),
        ],
        out_specs=pl.BlockSpec((t, MLA_V_DIM), lambda b, h: (b, h)),
        out_shape=jax.ShapeDtypeStruct((tp, MLA_HEADS * MLA_V_DIM), BF16),
        scratch_shapes=[pltpu.VMEM((t, 2 * NOPE_DIM), BF16)],
        compiler_params=_cparams(("parallel", "parallel")), name="mla_prompt",
    )(q, kvu, kvu, kr)


def _cumsum_kernel(t, lf_ref, tri_ref, c_ref):
    carry = jnp.zeros((1, LANES), F32)
    tri = tri_ref[...]
    for c in range(t // CHUNK):
        x = lf_ref[c * CHUNK:(c + 1) * CHUNK, :]
        x1 = x.astype(BF16)
        r1 = x - x1.astype(F32)
        x2 = r1.astype(BF16)
        x3 = (r1 - x2.astype(F32)).astype(BF16)
        cs = _dot(tri, x1) + _dot(tri, x2) + _dot(tri, x3) + carry
        c_ref[c * CHUNK:(c + 1) * CHUNK, :] = cs
        carry = cs[CHUNK - 1:CHUNK, :]


def _fox_cumsum(lf, tri, n_batch, t):
    return pl.pallas_call(
        functools.partial(_cumsum_kernel, t),
        grid=(n_batch,),
        in_specs=[pl.BlockSpec((t, LANES), lambda b: (b, 0)), _resident((CHUNK, CHUNK))],
        out_specs=pl.BlockSpec((t, LANES), lambda b: (b, 0)),
        out_shape=jax.ShapeDtypeStruct((n_batch * t, LANES), F32),
        compiler_params=_cparams(("parallel",)), name="fox_cumsum",
    )(lf, tri)


def _fox_prompt_kernel(t, blk, q_ref, k_ref, v_ref, cq_ref, ck_ref, o_ref):
    for i in range(2):
        def store(q0, o, i=i):
            if i == 0:
                o_ref[q0:q0 + blk, :] = o.astype(BF16)
            else:
                prev = o_ref[q0:q0 + blk, :]
                o_ref[q0:q0 + blk, :] = prev + pltpu.roll(o, LANES // 2, 1).astype(BF16)

        _attend_causal(t, blk, lambda q0, i=i: q_ref[i, q0:q0 + blk, :], k_ref.at[0], v_ref.at[0], store,
                       cq_ref=cq_ref.at[0, i], ck_ref=ck_ref.at[0, i])


def _fox_prompt(fq3, fk3, fv3, c_col, c_row, n_batch, t, blk):
    tp = n_batch * t
    pairs_per_kv = FOX_GROUP // 2
    return pl.pallas_call(
        functools.partial(_fox_prompt_kernel, t, blk),
        grid=(n_batch, FOX_HEADS // 2),
        in_specs=[
            pl.BlockSpec((2, t, LANES), lambda b, j: (j, b, 0)),
            pl.BlockSpec((1, t, LANES), lambda b, j: (j // pairs_per_kv, b, 0)),
            pl.BlockSpec((1, t, LANES), lambda b, j: (j // pairs_per_kv, b, 0)),
            pl.BlockSpec((1, 2, t, 1), lambda b, j: (b, j, 0, 0)),
            pl.BlockSpec((1, 2, 1, t), lambda b, j: (b, j, 0, 0)),
        ],
        out_specs=pl.BlockSpec((t, LANES), lambda b, j: (b, j)),
        out_shape=jax.ShapeDtypeStruct((tp, FOX_WIDTH), BF16),
        compiler_params=_cparams(("parallel", "parallel")), name="fox_prompt",
    )(fq3, fk3, fv3, c_col, c_row)


def _heads_mm_kernel(n_heads, kh, nh, x_ref, w_ref, o_ref):
    for h in range(n_heads):
        o_ref[:, h * nh:(h + 1) * nh] = _dot(x_ref[:, h * kh:(h + 1) * kh].astype(BF16), w_ref[h]).astype(o_ref.dtype)


def _heads_mm(x, w3, out_dtype, name):
    m = x.shape[0]
    n_heads, kh, nh = w3.shape
    return pl.pallas_call(
        functools.partial(_heads_mm_kernel, n_heads, kh, nh),
        in_specs=[pl.BlockSpec(memory_space=pltpu.VMEM), pl.BlockSpec(memory_space=pltpu.VMEM)],
        out_specs=pl.BlockSpec(memory_space=pltpu.VMEM),
        out_shape=jax.ShapeDtypeStruct((m, n_heads * nh), out_dtype),
        compiler_params=pltpu.CompilerParams(vmem_limit_bytes=VMEM_LIMIT), name=name,
    )(x, w3)


def _page_copies(pt_ref, bb, slot, n_pages, page_size, layer, specs, sems):
    copies = []
    for p in range(n_pages):
        page = pt_ref[bb, p]
        for k, (hbm, buf, on_rows) in enumerate(specs):
            if on_rows:
                dst = buf.at[slot, pl.ds(p * page_size, page_size), :]
            else:
                dst = buf.at[slot, :, pl.ds(p * page_size, page_size)]
            copies.append(pltpu.make_async_copy(hbm.at[layer, page], dst, sems.at[slot, k]))
    return copies


def _prefetch_pages(pt_ref, n_pages, page_size, layer, specs, sems):
    b = pl.program_id(0)
    nb = pl.num_programs(0)

    @pl.when(b == 0)
    def _():
        for c in _page_copies(pt_ref, 0, 0, n_pages, page_size, layer, specs, sems):
            c.start()

    @pl.when(b + 1 < nb)
    def _():
        for c in _page_copies(pt_ref, b + 1, (b + 1) % 2, n_pages, page_size, layer, specs, sems):
            c.start()

    slot = b % 2
    for c in _page_copies(pt_ref, b, slot, n_pages, page_size, layer, specs, sems):
        c.wait()
    return slot


def _mla_decode_kernel(layer, n_pages, page_size, pt_ref,
                       qa_ref, qr_ref, cnew_ref, krnew_ref, ckv_hbm, krt_hbm,
                       o_ref, ckv_buf, kr_buf, ckv_bf, sems):
    slot = _prefetch_pages(pt_ref, n_pages, page_size, layer,
                           [(ckv_hbm, ckv_buf, True), (krt_hbm, kr_buf, False)], sems)
    ckv_bf[...] = ckv_buf[slot].astype(BF16)
    qa = qa_ref[0]
    qr = qr_ref[0]
    s = _dot_nt(qa, ckv_bf[...]) + _dot(qr[:, 0:ROPE_DIM], kr_buf[slot].astype(BF16))
    cn = cnew_ref[0].astype(BF16).astype(F32)
    kn = krnew_ref[0].astype(BF16).astype(F32)
    s_new = (jnp.sum(qa.astype(F32) * cn, axis=1, keepdims=True)
             + jnp.sum(qr.astype(F32) * kn, axis=1, keepdims=True))
    m = jnp.maximum(jnp.max(s, axis=1, keepdims=True), s_new)
    p = jnp.exp(s - m)
    p_new = jnp.exp(s_new - m)
    l = jnp.sum(p, axis=1, keepdims=True) + p_new
    o = _dot(p.astype(BF16), ckv_bf[...]) + p_new.astype(BF16).astype(F32) * cn
    o_ref[0] = o / l


def _mla_decode(layer, page_table, qa3, qr3, cnew3, krnew3, cache_kv, cache_krt):
    db, n_pages = page_table.shape
    page_size = cache_kv.shape[2]
    past = n_pages * page_size
    per_b = lambda *tail: pl.BlockSpec((1,) + tail, lambda b, pt: (b,) + (0,) * len(tail))
    grid_spec = pltpu.PrefetchScalarGridSpec(
        num_scalar_prefetch=1, grid=(db,),
        in_specs=[per_b(MLA_HEADS, KV_LORA), per_b(MLA_HEADS, LANES), per_b(1, KV_LORA), per_b(1, LANES),
                  pl.BlockSpec(memory_space=pl.ANY), pl.BlockSpec(memory_space=pl.ANY)],
        out_specs=per_b(MLA_HEADS, KV_LORA),
        scratch_shapes=[pltpu.VMEM((2, past, KV_LORA), F32), pltpu.VMEM((2, ROPE_DIM, past), F32),
                        pltpu.VMEM((past, KV_LORA), BF16), pltpu.SemaphoreType.DMA((2, 2))],
    )
    return pl.pallas_call(
        functools.partial(_mla_decode_kernel, layer, n_pages, page_size),
        grid_spec=grid_spec,
        out_shape=jax.ShapeDtypeStruct((db, MLA_HEADS, KV_LORA), F32),
        compiler_params=_cparams(("arbitrary",)), name="mla_decode",
    )(page_table, qa3, qr3, cnew3, krnew3, cache_kv, cache_krt)


def _suffix_sum_lanes(x):
    n = x.shape[1]
    lane = lax.broadcasted_iota(I32, x.shape, 1)
    sh = 1
    while sh < n:
        shifted = pltpu.roll(x, n - sh, 1)
        x = x + jnp.where(lane < n - sh, shifted, 0.0)
        sh *= 2
    return x


def _fox_decode_kernel(layer, n_pages, page_size, pt_ref,
                       q_ref, knew_ref, vnew_ref, lfnew_ref, kt_hbm, vt_hbm, lft_hbm,
                       o_ref, k_buf, v_buf, lf_buf, sems):
    slot = _prefetch_pages(pt_ref, n_pages, page_size, layer,
                           [(kt_hbm, k_buf, False), (vt_hbm, v_buf, False), (lft_hbm, lf_buf, False)], sems)
    q = q_ref[0]
    lf = lf_buf[slot]
    bias = _suffix_sum_lanes(lf) - lf + lfnew_ref[0]
    s = _dot(q, k_buf[slot].astype(BF16)) + bias
    kn = knew_ref[0].astype(BF16).astype(F32)
    vn = vnew_ref[0].astype(BF16).astype(F32)
    s_new = jnp.sum(q.astype(F32) * kn, axis=1, keepdims=True)
    m = jnp.maximum(jnp.max(s, axis=1, keepdims=True), s_new)
    p = jnp.exp(s - m)
    p_new = jnp.exp(s_new - m)
    l = jnp.sum(p, axis=1, keepdims=True) + p_new
    o = _dot_nt(p.astype(BF16), v_buf[slot].astype(BF16)) + p_new.astype(BF16).astype(F32) * vn
    o_ref[0] = o / l


def _fox_decode(layer, page_table, qbd3, knew3, vnew3, lfnew3, cache_kt, cache_vt, cache_lft):
    db, n_pages = page_table.shape
    page_size = cache_kt.shape[3]
    past = n_pages * page_size
    per_b = lambda *tail: pl.BlockSpec((1,) + tail, lambda b, pt: (b,) + (0,) * len(tail))
    grid_spec = pltpu.PrefetchScalarGridSpec(
        num_scalar_prefetch=1, grid=(db,),
        in_specs=[per_b(FOX_HEADS, LANES), per_b(1, LANES), per_b(1, LANES), per_b(FOX_HEADS, 1),
                  pl.BlockSpec(memory_space=pl.ANY), pl.BlockSpec(memory_space=pl.ANY),
                  pl.BlockSpec(memory_space=pl.ANY)],
        out_specs=per_b(FOX_HEADS, LANES),
        scratch_shapes=[pltpu.VMEM((2, FOX_KV_WIDTH, past), F32), pltpu.VMEM((2, FOX_KV_WIDTH, past), F32),
                        pltpu.VMEM((2, FOX_HEADS, past), F32), pltpu.SemaphoreType.DMA((2, 3))],
    )
    return pl.pallas_call(
        functools.partial(_fox_decode_kernel, layer, n_pages, page_size),
        grid_spec=grid_spec,
        out_shape=jax.ShapeDtypeStruct((db, FOX_HEADS, LANES), F32),
        compiler_params=_cparams(("arbitrary",)), name="fox_decode",
    )(page_table, qbd3, knew3, vnew3, lfnew3, cache_kt, cache_vt, cache_lft)


def _group_top2(sel, sc):
    idx = lax.broadcasted_iota(I32, sel.shape, 0).astype(F32)
    big = float(EXPERTS_PER_GROUP)
    m1 = jnp.max(sel, axis=0, keepdims=True)
    i1 = jnp.min(jnp.where(sel == m1, idx, big), axis=0, keepdims=True)
    rest = jnp.where(idx == i1, -jnp.inf, sel)
    m2 = jnp.max(rest, axis=0, keepdims=True)
    i2 = jnp.min(jnp.where(rest == m2, idx, big), axis=0, keepdims=True)
    s1 = jnp.sum(jnp.where(idx == i1, sc, 0.0), axis=0, keepdims=True)
    s2 = jnp.sum(jnp.where(idx == i2, sc, 0.0), axis=0, keepdims=True)
    return m1 + m2, i1, i2, s1, s2


def _router_logits(rwt, x1):
    x_hi = x1.astype(BF16)
    x_lo = (x1 - x_hi.astype(F32)).astype(BF16)
    w_hi = rwt.astype(BF16)
    w_lo = (rwt - w_hi.astype(F32)).astype(BF16)
    return _dot_nt(w_hi, x_hi) + _dot_nt(w_hi, x_lo) + _dot_nt(w_lo, x_hi)


def _outproj_kernel(alpha, tm, x_ref, om_ref, og_ref, of_ref, wo_ref, g_ref, b_ref, rwt_ref, rb_ref, tri_ref,
                    x1_ref, x1r_ref, route_ref, cnt_ref, carry_ref):
    k1 = om_ref.shape[1]
    k2 = k1 + og_ref.shape[1]
    acc = (_dot(om_ref[...], wo_ref[0:k1, :]) + _dot(og_ref[...], wo_ref[k1:k2, :])
           + _dot(of_ref[...], wo_ref[k2:, :]))
    x1 = _layer_norm(alpha * x_ref[...] + acc, g_ref[...], b_ref[...])
    x1_ref[...] = x1
    x1r_ref[...] = x1.astype(BF16).reshape(x1r_ref.shape)

    sc = _sigmoid(_router_logits(rwt_ref[...], x1))
    sel = sc + rb_ref[...]
    best = None
    for g in range(N_GROUPS):
        r0 = g * EXPERTS_PER_GROUP
        gs, i1, i2, s1, s2 = _group_top2(sel[r0:r0 + EXPERTS_PER_GROUP], sc[r0:r0 + EXPERTS_PER_GROUP])
        cand = (gs, i1 + float(r0), i2 + float(r0), s1, s2)
        if best is None:
            best = cand
        else:
            better = gs > best[0]
            best = tuple(jnp.where(better, c, o) for c, o in zip(cand, best))
    _, e1, e2, s1, s2 = best
    denom = s1 + s2
    eidx = lax.broadcasted_iota(I32, (N_EXPERTS, tm), 0).astype(F32)
    hit1 = eidx == e1
    hit2 = eidx == e2
    onehot = jnp.where(hit1 | hit2, 1.0, 0.0)

    @pl.when(pl.program_id(0) == 0)
    def _():
        carry_ref[...] = jnp.zeros_like(carry_ref)

    before = _dot(onehot.astype(BF16), tri_ref[...]) + carry_ref[:, 0:1]
    r1 = jnp.sum(jnp.where(hit1, before, 0.0), axis=0, keepdims=True)
    r2 = jnp.sum(jnp.where(hit2, before, 0.0), axis=0, keepdims=True)
    carry_ref[...] = carry_ref[...] + jnp.sum(onehot, axis=1, keepdims=True)
    cnt_ref[...] = carry_ref[...]
    zero = jnp.zeros_like(e1)
    route_ref[...] = jnp.concatenate([e1, e2, s1 / denom, s2 / denom, r1, r2, zero, zero], axis=0)


def _outproj(alpha, h, o_mla, o_gm, o_fox, lw, rwt, rb, tri, tm):
    tt, d = h.shape
    row = lambda w, dt=None: pl.BlockSpec((tm, w), lambda i: (i, 0))
    return pl.pallas_call(
        functools.partial(_outproj_kernel, alpha, tm),
        grid=(tt // tm,),
        in_specs=[row(d), row(o_mla.shape[1]), row(o_gm.shape[1]), row(o_fox.shape[1]),
                  _resident(lw["w_out"].shape), _resident((1, d)), _resident((1, d)),
                  _resident((N_EXPERTS, d)), _resident((N_EXPERTS, 1)), _resident((tm, tm))],
        out_specs=(row(d), pl.BlockSpec((tm, d // LANES, LANES), lambda i: (i, 0, 0)),
                   pl.BlockSpec((SUBLANES, tm), lambda i: (0, i)),
                   pl.BlockSpec((N_EXPERTS, LANES), lambda i: (0, 0))),
        out_shape=(jax.ShapeDtypeStruct((tt, d), F32), jax.ShapeDtypeStruct((tt, d // LANES, LANES), BF16),
                   jax.ShapeDtypeStruct((SUBLANES, tt), F32), jax.ShapeDtypeStruct((N_EXPERTS, LANES), F32)),
        scratch_shapes=[pltpu.VMEM((N_EXPERTS, LANES), F32)],
        compiler_params=_cparams(("arbitrary",)), name="outproj_router",
    )(h, o_mla, o_gm, o_fox, lw["w_out"], lw["ln1_g"], lw["ln1_b"], rwt, rb, tri)


def _rowcopy_kernel(n, window, src_ref, dst_ref, in_hbm, *rest):
    out_hbm, sem = rest[-2:]

    def row_copy(s, d):
        return pltpu.make_async_copy(in_hbm.at[pl.ds(s, 1)], out_hbm.at[pl.ds(d, 1)], sem)

    def body(i, carry):
        row_copy(src_ref[i], dst_ref[i]).start()

        @pl.when(i >= window)
        def _():
            row_copy(0, 0).wait()

        return carry

    lax.fori_loop(0, n, body, 0)

    def drain(i, carry):
        row_copy(0, 0).wait()
        return carry

    lax.fori_loop(0, min(n, window), drain, 0)


def _rowcopy(src_idx, dst_idx, x, out_rows, out_init=None):
    n = src_idx.shape[0]
    out_shape = (out_rows,) + x.shape[1:]
    extra = [] if out_init is None else [out_init]
    grid_spec = pltpu.PrefetchScalarGridSpec(
        num_scalar_prefetch=2, grid=(1,),
        in_specs=[pl.BlockSpec(memory_space=pl.ANY)] * (1 + len(extra)),
        out_specs=pl.BlockSpec(memory_space=pl.ANY),
        scratch_shapes=[pltpu.SemaphoreType.DMA(())],
    )
    return pl.pallas_call(
        functools.partial(_rowcopy_kernel, n, DMA_WINDOW),
        grid_spec=grid_spec,
        out_shape=jax.ShapeDtypeStruct(out_shape, x.dtype),
        input_output_aliases={} if out_init is None else {3: 0},
        compiler_params=pltpu.CompilerParams(dimension_semantics=("arbitrary",), has_side_effects=True),
        name="rowcopy",
    )(src_idx, dst_idx, x, *extra)


def _moe_kernel(te_ref, first_ref, nused_ref, x_ref, wg_ref, wu_ref, wd_ref, y_ref, wg_bf, wu_bf, wd_bf):
    i = pl.program_id(0)
    tile = x_ref.shape[0]

    @pl.when(first_ref[i] == 1)
    def _():
        wg_bf[...] = wg_ref[...].astype(BF16)
        wu_bf[...] = wu_ref[...].astype(BF16)
        wd_bf[...] = wd_ref[...].astype(BF16)

    @pl.when(i < nused_ref[0])
    def _():
        x = x_ref[...].reshape(tile, wg_bf.shape[0])
        gate = _dot(x, wg_bf[...])
        a = gate * _sigmoid(gate) * _dot(x, wu_bf[...])
        y_ref[...] = _dot(a.astype(BF16), wd_bf[...]).astype(BF16).reshape(y_ref.shape)

    @pl.when(i >= nused_ref[0])
    def _():
        y_ref[...] = jnp.zeros_like(y_ref)


def _moe_gemm(layer, te, first, nused, xs, w_gate, w_up, w_down, tile):
    p = xs.shape[0]
    d, f = w_gate.shape[-2:]
    nt = p // tile
    last = lambda i, nu: jnp.minimum(i, nu[0] - 1)
    grid_spec = pltpu.PrefetchScalarGridSpec(
        num_scalar_prefetch=3, grid=(nt,),
        in_specs=[
            pl.BlockSpec((tile,) + xs.shape[1:], lambda i, te, fi, nu: (last(i, nu), 0, 0)),
            pl.BlockSpec((None, None, d, f), lambda i, te, fi, nu: (layer, te[i], 0, 0)),
            pl.BlockSpec((None, None, d, f), lambda i, te, fi, nu: (layer, te[i], 0, 0)),
            pl.BlockSpec((None, None, f, d), lambda i, te, fi, nu: (layer, te[i], 0, 0)),
        ],
        out_specs=pl.BlockSpec((tile,) + xs.shape[1:], lambda i, te, fi, nu: (i, 0, 0)),
        scratch_shapes=[pltpu.VMEM((d, f), BF16), pltpu.VMEM((d, f), BF16), pltpu.VMEM((f, d), BF16)],
    )
    return pl.pallas_call(
        _moe_kernel, grid_spec=grid_spec,
        out_shape=jax.ShapeDtypeStruct(xs.shape, BF16),
        compiler_params=_cparams(("arbitrary",)), name="moe_gemm",
    )(te, first, nused, xs, w_gate, w_up, w_down)


def _moe(layer, x1r, route, counts, w_gate, w_up, w_down):
    tt = x1r.shape[0]
    tile = MOE_TILE
    nt = -(-2 * tt // tile) + N_EXPERTS
    p = nt * tile
    e1 = route[0].astype(I32)
    e2 = route[1].astype(I32)
    cnt = counts[:, 0].astype(I32)
    padded = ((cnt + tile - 1) // tile) * tile
    ends = jnp.cumsum(padded)
    starts = ends - padded
    pos1 = starts[e1] + route[4].astype(I32)
    pos2 = starts[e2] + route[5].astype(I32)
    nused = (ends[-1] // tile).astype(I32)
    tile_id = jnp.minimum(jnp.arange(nt, dtype=I32), nused - 1)
    te = jnp.sum((ends[None, :] <= (tile_id * tile)[:, None]).astype(I32), axis=1)
    te = jnp.minimum(te, N_EXPERTS - 1)
    first = jnp.concatenate([jnp.ones((1,), I32), (te[1:] != te[:-1]).astype(I32)])
    tok = jnp.arange(tt, dtype=I32)
    src = jnp.concatenate([tok, tok])
    dst = jnp.concatenate([pos1, pos2])
    xs = _rowcopy(src, dst, x1r, p, jnp.zeros((p,) + x1r.shape[1:], x1r.dtype))
    ys = _moe_gemm(layer, te, first, nused.reshape(1), xs, w_gate, w_up, w_down, tile)
    back = _rowcopy(dst, jnp.arange(2 * tt, dtype=I32), ys, 2 * tt)
    return back.reshape((2, tt) + x1r.shape[1:])


def _finish_kernel(alpha, x1_ref, y_ref, gw_ref, p_ref, wpe_ref, wpg_ref, g2_ref, b2_ref, g3_ref, b3_ref,
                   h_ref, hb_ref):
    gw = gw_ref[...]
    y0 = y_ref[0].reshape(x1_ref.shape).astype(F32)
    y1 = y_ref[1].reshape(x1_ref.shape).astype(F32)
    moe = y0 * gw[:, 2:3] + y1 * gw[:, 3:4]
    x2 = _layer_norm(alpha * x1_ref[...] + moe, g2_ref[...], b2_ref[...])
    e = _dot(p_ref[...].astype(BF16), wpe_ref[...]) * _sigmoid(_dot(x2.astype(BF16), wpg_ref[...]))
    x3 = _layer_norm(alpha * x2 + e, g3_ref[...], b3_ref[...])
    h_ref[...] = x3
    hb_ref[...] = x3.astype(BF16)


def _finish(alpha, x1, y2, route_t, p_l, lw, tm):
    tt, d = x1.shape
    row = lambda w: pl.BlockSpec((tm, w), lambda i: (i, 0))
    return pl.pallas_call(
        functools.partial(_finish_kernel, alpha),
        grid=(tt // tm,),
        in_specs=[row(d), pl.BlockSpec((2, tm) + y2.shape[2:], lambda i: (0, i, 0, 0)), row(SUBLANES),
                  row(p_l.shape[1]),
                  _resident(lw["w_pe"].shape), _resident(lw["w_pg"].shape),
                  _resident((1, d)), _resident((1, d)), _resident((1, d)), _resident((1, d))],
        out_specs=(row(d), row(d)),
        out_shape=(jax.ShapeDtypeStruct((tt, d), F32), jax.ShapeDtypeStruct((tt, d), BF16)),
        compiler_params=_cparams(("parallel",)), name="finish",
    )(x1, y2, route_t, p_l, lw["w_pe"], lw["w_pg"], lw["ln2_g"], lw["ln2_b"], lw["ln3_g"], lw["ln3_b"])


def _pack_layer_weights(l, w_in, q_norm_g, kv_norm_g, w_uq, w_ukv, gmlp_ln_g, gmlp_ln_b, gmlp_ws, gmlp_bs, fox_bf,
                        w_out, ln1_g, ln1_b, ln2_g, ln2_b, ln3_g, ln3_b, w_pe, w_pg):
    d = w_in.shape[1]
    wi = w_in[l]
    splits = np.cumsum([Q_LORA, KV_LORA, ROPE_DIM, GM_WIDTH, GM_WIDTH, FOX_WIDTH, FOX_KV_WIDTH, FOX_KV_WIDTH])
    q_lat, kv_lat, k_r, u, v, fq, fk, fv, fl = jnp.split(wi, [int(s) for s in splits], axis=1)
    w_in_p = jnp.concatenate(
        [q_lat, kv_lat, k_r, k_r, u, v, fq, fk, fv, fl, jnp.zeros((d, LANES - FOX_HEADS), F32)], axis=1)
    uq = w_uq[l].reshape(Q_LORA, MLA_HEADS, NOPE_DIM + ROPE_DIM)
    w_uq_p = jnp.concatenate([uq, uq[:, :, NOPE_DIM:]], axis=2).reshape(Q_LORA, MLA_HEADS * Q_HEAD_COLS)
    ukv = w_ukv[l].reshape(KV_LORA, MLA_HEADS, NOPE_DIM + MLA_V_DIM)
    w_uk_t = jnp.transpose(ukv[:, :, :NOPE_DIM], (1, 2, 0))
    w_uv = jnp.transpose(ukv[:, :, NOPE_DIM:], (1, 0, 2))
    ws = gmlp_ws[l]
    tril = jnp.tril(jnp.ones((CHUNK, CHUNK), dtype=bool))
    gm_w = jnp.stack([jnp.where(tril, ws, 0.0),
                      ws[:, 0:1, 0:1] * jnp.eye(CHUNK, dtype=F32)[None]])
    bs = gmlp_bs[l]
    bias_t = jnp.repeat(bs.T, GM_CH, axis=1)
    bias_s = jnp.broadcast_to(jnp.repeat(bs[:, 0], GM_CH)[None, :], (CHUNK, GM_WIDTH))
    bf = jnp.concatenate([fox_bf[l], jnp.zeros((LANES - FOX_HEADS,), F32)])[None, :]
    r = lambda a: a[l][None, :]
    return dict(
        w_in=w_in_p.astype(BF16), q_norm_g=r(q_norm_g), kv_norm_g=r(kv_norm_g), w_uq=w_uq_p.astype(BF16),
        w_ukv=w_ukv[l].astype(BF16), w_uk_t=w_uk_t.astype(BF16), w_uv=w_uv.astype(BF16),
        gm_ln_g=r(gmlp_ln_g), gm_ln_b=r(gmlp_ln_b), gm_w=gm_w.astype(BF16), gm_bias=jnp.stack([bias_t, bias_s]),
        fox_bf=bf, w_out=w_out[l].astype(BF16), ln1_g=r(ln1_g), ln1_b=r(ln1_b), ln2_g=r(ln2_g), ln2_b=r(ln2_b),
        ln3_g=r(ln3_g), ln3_b=r(ln3_b), w_pe=w_pe[l].astype(BF16), w_pg=w_pg[l].astype(BF16))


def _row_tile(tt):
    for tm in range(ROW_TILE_CAP, 0, -CHUNK):
        if tt % tm == 0:
            return tm
    raise ValueError("stacked row count must be a multiple of the gMLP chunk")


def kernel(x_prompt, x_sample, cache_mla_kv, cache_mla_krope, cache_fox_k, cache_fox_v, cache_fox_logf, page_table, p_prompt, p_sample, w_in, q_norm_g, kv_norm_g, w_uq, w_ukv, gmlp_ln_g, gmlp_ln_b, gmlp_ws, gmlp_bs, fox_bf, w_out, ln1_g, ln1_b, ln2_g, ln2_b, ln3_g, ln3_b, router_w, router_b, w_e_gate, w_e_up, w_e_down, w_pe, w_pg):
    n_batch, t, d = x_prompt.shape
    db, dec_seq, _ = x_sample.shape
    depth = w_in.shape[0]
    n_pages = page_table.shape[1]
    page_size = cache_mla_kv.shape[2]
    past = n_pages * page_size
    assert dec_seq == 1 and t % CHUNK == 0 and db % CHUNK == 0
    tp = n_batch * t
    tt = tp + db
    tm = _row_tile(tt)
    tm_fin = tm // 2 if (tm // 2) % SUBLANES == 0 else tm
    blk = min(ATT_BLOCK, t)
    alpha = (2 * depth) ** 0.25

    pos = jnp.concatenate([jnp.tile(jnp.arange(t), n_batch), jnp.full((db,), past)]).astype(F32)
    half = ROPE_DIM // 2
    inv = ROPE_BASE ** (-jnp.arange(half, dtype=F32) / half)
    ang = pos[:, None] * inv[None, :]
    cos_t = jnp.tile(jnp.cos(ang), (1, LANES // half))
    sin_h = jnp.sin(ang)
    sin_t = jnp.tile(jnp.concatenate([-sin_h, sin_h], axis=1), (1, LANES // ROPE_DIM))

    cache_krt = jnp.transpose(cache_mla_krope, (0, 1, 3, 2))
    n_pool = cache_fox_k.shape[1]
    cache_kt = jnp.transpose(cache_fox_k, (0, 1, 3, 4, 2)).reshape(depth, n_pool, FOX_KV_WIDTH, page_size)
    cache_vt = jnp.transpose(cache_fox_v, (0, 1, 3, 4, 2)).reshape(depth, n_pool, FOX_KV_WIDTH, page_size)
    cache_lft = jnp.transpose(cache_fox_logf, (0, 1, 3, 2))

    tri_lower = jnp.tril(jnp.ones((CHUNK, CHUNK), F32)).astype(BF16)
    tri_before = jnp.triu(jnp.ones((tm, tm), F32), k=1).astype(BF16)
    rwt = router_w.T
    rb = router_b[:, None]

    h = jnp.concatenate([x_prompt.reshape(tp, d), x_sample.reshape(db, d)], axis=0)
    hb = h.astype(BF16)
    outs = [[] for _ in range(11)]
    for l in range(depth):
        lw = _pack_layer_weights(l, w_in, q_norm_g, kv_norm_g, w_uq, w_ukv, gmlp_ln_g, gmlp_ln_b, gmlp_ws, gmlp_bs,
                                 fox_bf, w_out, ln1_g, ln1_b, ln2_g, ln2_b, ln3_g, ln3_b, w_pe, w_pg)
        q, ckv, kr, kvu, gv, go, fq3, fk, fv, fk3, fv3, lf = _inproj(hb, lw, cos_t, sin_t, tp // CHUNK, tm)

        o_mla_p = _mla_prompt(q, kvu, kr, n_batch, t, blk)
        c = _fox_cumsum(lf, tri_lower, n_batch, t)[:, :FOX_HEADS].reshape(n_batch, t, FOX_HEADS)
        c_hm = jnp.transpose(c, (0, 2, 1))
        o_fox_p = _fox_prompt(fq3, fk3, fv3, c_hm[..., None], c_hm[:, :, None, :], n_batch, t, blk)

        q_s = q[tp:].reshape(db, MLA_HEADS, Q_HEAD_COLS)
        qa = _heads_mm(q_s[:, :, :NOPE_DIM].reshape(db, MLA_HEADS * NOPE_DIM), lw["w_uk_t"], BF16, "mla_absorb_q")
        o_lat = _mla_decode(l, page_table, qa.reshape(db, MLA_HEADS, KV_LORA), q_s[:, :, NOPE_DIM:],
                            ckv[tp:, None, :], kr[tp:, None, :], cache_mla_kv, cache_krt)
        o_mla_s = _heads_mm(o_lat.reshape(db, MLA_HEADS * KV_LORA), lw["w_uv"], BF16, "mla_absorb_v")
        fq_s = fq3[:, tp:, :FOX_HEAD_DIM]
        zeros_q = jnp.zeros_like(fq_s)
        kvh = (jnp.arange(FOX_HEADS) // FOX_GROUP)[:, None, None]
        qbd = jnp.concatenate([jnp.where(kvh == 0, fq_s, zeros_q), jnp.where(kvh == 1, fq_s, zeros_q)], axis=2)
        o8 = _fox_decode(l, page_table, jnp.transpose(qbd, (1, 0, 2)), fk[tp:, None, :], fv[tp:, None, :],
                         lf[tp:, :FOX_HEADS, None], cache_kt, cache_vt, cache_lft)
        o8 = o8.reshape(db, FOX_KV_HEADS, FOX_GROUP, FOX_KV_HEADS, FOX_HEAD_DIM)
        o_fox_s = jnp.stack([o8[:, n, :, n, :] for n in range(FOX_KV_HEADS)], axis=1).reshape(db, FOX_WIDTH)

        o_mla = jnp.concatenate([o_mla_p, o_mla_s], axis=0)
        o_fox = jnp.concatenate([o_fox_p, o_fox_s.astype(BF16)], axis=0)
        x1, x1r, route, counts = _outproj(alpha, h, o_mla, go, o_fox, lw, rwt, rb, tri_before, tm)
        y2 = _moe(l, x1r, route, counts, w_e_gate, w_e_up, w_e_down)
        p_l = jnp.concatenate([p_prompt[l].reshape(tp, -1), p_sample[l].reshape(db, -1)], axis=0)
        h, hb = _finish(alpha, x1, y2, route.T, p_l, lw, tm_fin)

        leaves = (ckv[:tp].reshape(n_batch, t, KV_LORA), kr[:tp, :ROPE_DIM].reshape(n_batch, t, ROPE_DIM),
                  fk[:tp].reshape(n_batch, t, FOX_KV_HEADS, FOX_HEAD_DIM),
                  fv[:tp].reshape(n_batch, t, FOX_KV_HEADS, FOX_HEAD_DIM),
                  lf[:tp, :FOX_HEADS].reshape(n_batch, t, FOX_HEADS),
                  ckv[tp:].reshape(db, 1, KV_LORA), kr[tp:, :ROPE_DIM].reshape(db, 1, ROPE_DIM),
                  fk[tp:].reshape(db, 1, FOX_KV_HEADS, FOX_HEAD_DIM), fv[tp:].reshape(db, 1, FOX_KV_HEADS, FOX_HEAD_DIM),
                  lf[tp:, :FOX_HEADS].reshape(db, 1, FOX_HEADS), gv[tp:].reshape(db, 1, GM_WIDTH))
        for acc, leaf in zip(outs, leaves):
            acc.append(leaf)
    return (h[:tp].reshape(n_batch, t, d), h[tp:].reshape(db, 1, d)) + tuple(jnp.stack(o) for o in outs)
```

```python
import functools

import jax
import jax.numpy as jnp
import numpy as np
from jax import lax
from jax.experimental import pallas as pl
from jax.experimental.pallas import tpu as pltpu

F32 = jnp.float32
BF16 = jnp.bfloat16
I32 = jnp.int32

MLA_HEADS = 8
Q_LORA = 512
KV_LORA = 256
NOPE_DIM = 128
ROPE_DIM = 64
MLA_V_DIM = 128
ROPE_BASE = 10000.0
MLA_SCALE = (NOPE_DIM + ROPE_DIM) ** -0.5
GM_GROUPS = 4
GM_CH = 128
GM_WIDTH = GM_GROUPS * GM_CH
CHUNK = 128
FOX_HEADS = 8
FOX_KV_HEADS = 2
FOX_GROUP = FOX_HEADS // FOX_KV_HEADS
FOX_HEAD_DIM = 64
FOX_WIDTH = FOX_HEADS * FOX_HEAD_DIM
FOX_KV_WIDTH = FOX_KV_HEADS * FOX_HEAD_DIM
FOX_SCALE = FOX_HEAD_DIM ** -0.5
N_EXPERTS = 32
N_GROUPS = 4
EXPERTS_PER_GROUP = N_EXPERTS // N_GROUPS
D_EXPERT = 512
LN_EPS = 1e-5
RMS_EPS = 1e-6
NEG_INF = -1e30

LANES = 128
SUBLANES = 8
VMEM_LIMIT = 56 * 1024 * 1024

ROW_TILE_CAP = 640
ATT_BLOCK = 256
MOE_TILE = 256
GATHER_ROWS = 1024
GATHER_ROWS_CAP = 1280
ROW_DMA_UNROLL = 8

C_QLAT = 0
C_KVLAT = C_QLAT + Q_LORA
C_KR = C_KVLAT + KV_LORA
C_U = C_KR + LANES
C_V = C_U + GM_WIDTH
C_FQ = C_V + GM_WIDTH
C_FK = C_FQ + FOX_WIDTH
C_FV = C_FK + FOX_KV_WIDTH
C_FL = C_FV + FOX_KV_WIDTH
IN_COLS_P = C_FL + LANES
Q_HEAD_COLS = 2 * LANES


def _cparams(sem, **kw):
    return pltpu.CompilerParams(dimension_semantics=sem, vmem_limit_bytes=VMEM_LIMIT, **kw)


def _dot(a, b):
    return jnp.dot(a, b, preferred_element_type=F32)


def _dot_nt(a, b):
    return lax.dot_general(a, b, (((1,), (1,)), ((), ())), preferred_element_type=F32)


def _layer_norm(x, g, b):
    mu = jnp.mean(x, axis=-1, keepdims=True)
    xc = x - mu
    var = jnp.mean(xc * xc, axis=-1, keepdims=True)
    return xc * lax.rsqrt(var + LN_EPS) * g + b


def _rms_norm(x, g):
    return x * lax.rsqrt(jnp.mean(x * x, axis=-1, keepdims=True) + RMS_EPS) * g


def _gelu_tanh(x):
    return 0.5 * x * (1.0 + jnp.tanh(0.7978845608028654 * (x + 0.044715 * (x * x * x))))


def _log_sigmoid(x):
    return jnp.minimum(x, 0.0) - jnp.log1p(jnp.exp(-jnp.abs(x)))


def _sigmoid(x):
    return 1.0 / (1.0 + jnp.exp(-x))


def _resident(shape):
    nd = len(shape)
    return pl.BlockSpec(shape, lambda *_: (0,) * nd, pipeline_mode=pl.Buffered(1))


def _inproj_kernel(n_prompt_chunks, tm,
                   hb_ref, win_ref, qg_ref, kvg_ref, wuq_ref, wukv_ref, cos_ref, sin_ref,
                   gg_ref, gb_ref, bf_ref, gw_ref, gbias_ref,
                   q_ref, ckv_ref, kr_ref, kvu_ref, gv_ref, go_ref, fq3_ref, fk_ref, fv_ref,
                   fk3_ref, fv3_ref, lf_ref):
    hb = hb_ref[...]

    def z(c0, width):
        return _dot(hb, win_ref[:, c0:c0 + width])

    cos = cos_ref[...]
    sin = sin_ref[...]
    lane = lax.broadcasted_iota(I32, (tm, LANES), 1)
    low_half = lane < (LANES // 2)

    def rope(blk):
        return blk * cos + pltpu.roll(blk, ROPE_DIM // 2, 1) * sin

    qn = _rms_norm(z(C_QLAT, Q_LORA), qg_ref[...])
    qf = _dot(qn.astype(BF16), wuq_ref[...])
    for h in range(MLA_HEADS):
        c0 = h * Q_HEAD_COLS
        q_ref[:, c0:c0 + LANES] = (qf[:, c0:c0 + LANES] * MLA_SCALE).astype(BF16)
        q_ref[:, c0 + LANES:c0 + 2 * LANES] = (rope(qf[:, c0 + LANES:c0 + 2 * LANES]) * MLA_SCALE).astype(BF16)

    ckv = _rms_norm(z(C_KVLAT, KV_LORA), kvg_ref[...])
    ckv_ref[...] = ckv
    kvu_ref[...] = _dot(ckv.astype(BF16), wukv_ref[...]).astype(BF16)
    kr_ref[...] = jnp.where(low_half, rope(z(C_KR, LANES)), 0.0)

    u = _gelu_tanh(z(C_U, GM_WIDTH))
    v = _layer_norm(_gelu_tanh(z(C_V, GM_WIDTH)), gg_ref[...], gb_ref[...])
    gv_ref[...] = v
    nch = tm // CHUNK
    for c in range(nch):
        is_sample = (pl.program_id(0) * nch + c) >= n_prompt_chunks
        r0 = c * CHUNK
        for g in range(GM_GROUPS):
            g0 = g * GM_CH
            w = jnp.where(is_sample, gw_ref[1, g], gw_ref[0, g])
            bias = jnp.where(is_sample, gbias_ref[1, :, g0:g0 + GM_CH], gbias_ref[0, :, g0:g0 + GM_CH])
            s = _dot(w, v[r0:r0 + CHUNK, g0:g0 + GM_CH].astype(BF16)) + bias
            go_ref[r0:r0 + CHUNK, g0:g0 + GM_CH] = (u[r0:r0 + CHUNK, g0:g0 + GM_CH] * s).astype(BF16)

    for j in range(FOX_HEADS // 2):
        pair = z(C_FQ + j * LANES, LANES) * FOX_SCALE
        fq3_ref[2 * j] = pair.astype(BF16)
        fq3_ref[2 * j + 1] = pltpu.roll(pair, LANES // 2, 1).astype(BF16)
    fk = z(C_FK, LANES)
    fv = z(C_FV, LANES)
    fk_ref[...] = fk
    fv_ref[...] = fv
    fk3_ref[0] = jnp.where(low_half, fk, 0.0).astype(BF16)
    fk3_ref[1] = jnp.where(low_half, pltpu.roll(fk, LANES // 2, 1), 0.0).astype(BF16)
    fv3_ref[0] = jnp.where(low_half, fv, 0.0).astype(BF16)
    fv3_ref[1] = jnp.where(low_half, pltpu.roll(fv, LANES // 2, 1), 0.0).astype(BF16)
    lf_ref[...] = _log_sigmoid(z(C_FL, LANES) + bf_ref[...])


def _inproj(hb, lw, cos_t, sin_t, n_prompt_chunks, tm):
    tt, d = hb.shape
    row = lambda w: pl.BlockSpec((tm, w), lambda i: (i, 0))
    out_shapes = (
        jax.ShapeDtypeStruct((tt, MLA_HEADS * Q_HEAD_COLS), BF16),
        jax.ShapeDtypeStruct((tt, KV_LORA), F32),
        jax.ShapeDtypeStruct((tt, LANES), F32),
        jax.ShapeDtypeStruct((tt, MLA_HEADS * (NOPE_DIM + MLA_V_DIM)), BF16),
        jax.ShapeDtypeStruct((tt, GM_WIDTH), F32),
        jax.ShapeDtypeStruct((tt, GM_WIDTH), BF16),
        jax.ShapeDtypeStruct((FOX_HEADS, tt, LANES), BF16),
        jax.ShapeDtypeStruct((tt, LANES), F32),
        jax.ShapeDtypeStruct((tt, LANES), F32),
        jax.ShapeDtypeStruct((FOX_KV_HEADS, tt, LANES), BF16),
        jax.ShapeDtypeStruct((FOX_KV_HEADS, tt, LANES), BF16),
        jax.ShapeDtypeStruct((tt, LANES), F32),
    )
    out_specs = (
        row(MLA_HEADS * Q_HEAD_COLS), row(KV_LORA), row(LANES), row(MLA_HEADS * (NOPE_DIM + MLA_V_DIM)),
        row(GM_WIDTH), row(GM_WIDTH),
        pl.BlockSpec((FOX_HEADS, tm, LANES), lambda i: (0, i, 0)),
        row(LANES), row(LANES),
        pl.BlockSpec((FOX_KV_HEADS, tm, LANES), lambda i: (0, i, 0)),
        pl.BlockSpec((FOX_KV_HEADS, tm, LANES), lambda i: (0, i, 0)),
        row(LANES),
    )
    in_specs = [
        row(d), _resident((d, IN_COLS_P)), _resident((1, Q_LORA)), _resident((1, KV_LORA)),
        _resident((Q_LORA, MLA_HEADS * Q_HEAD_COLS)), _resident((KV_LORA, MLA_HEADS * (NOPE_DIM + MLA_V_DIM))),
        row(LANES), row(LANES),
        _resident((1, GM_WIDTH)), _resident((1, GM_WIDTH)), _resident((1, LANES)),
        _resident((2, GM_GROUPS, CHUNK, CHUNK)), _resident((2, CHUNK, GM_WIDTH)),
    ]
    return pl.pallas_call(
        functools.partial(_inproj_kernel, n_prompt_chunks, tm),
        grid=(tt // tm,), in_specs=in_specs, out_specs=out_specs, out_shape=out_shapes,
        compiler_params=_cparams(("parallel",)), name="inproj",
    )(hb, lw["w_in"], lw["q_norm_g"], lw["kv_norm_g"], lw["w_uq"], lw["w_ukv"], cos_t, sin_t,
      lw["gm_ln_g"], lw["gm_ln_b"], lw["fox_bf"], lw["gm_w"], lw["gm_bias"])


def _attend_causal(t, blk, load_q, k_ref, v_ref, store_o, cq_ref=None, ck_ref=None):
    row = lax.broadcasted_iota(I32, (blk, blk), 0)
    col = lax.broadcasted_iota(I32, (blk, blk), 1)
    causal = col <= row
    for qi in range(t // blk):
        q0 = qi * blk
        e = q0 + blk
        s = _dot_nt(load_q(q0), k_ref[0:e, :])
        if cq_ref is not None:
            s = s + cq_ref[q0:e, :] - ck_ref[:, 0:e]
        s_d = jnp.where(causal, s[:, q0:e], NEG_INF)
        m = jnp.max(s_d, axis=1, keepdims=True)
        if qi > 0:
            s_o = s[:, 0:q0]
            m = jnp.maximum(m, jnp.max(s_o, axis=1, keepdims=True))
            p_o = jnp.exp(s_o - m)
        p_d = jnp.exp(s_d - m)
        l = jnp.sum(p_d, axis=1, keepdims=True)
        o = _dot(p_d.astype(BF16), v_ref[q0:e, :])
        if qi > 0:
            l = l + jnp.sum(p_o, axis=1, keepdims=True)
            o = o + _dot(p_o.astype(BF16), v_ref[0:q0, :])
        store_o(q0, o / l)


def _mla_prompt_kernel(t, blk, q_ref, kn_ref, v_ref, kr_ref, o_ref, kcat_ref):
    kcat_ref[:, 0:NOPE_DIM] = kn_ref[...]
    kcat_ref[:, NOPE_DIM:2 * NOPE_DIM] = kr_ref[...].astype(BF16)

    def store(q0, o):
        o_ref[q0:q0 + blk, :] = o.astype(BF16)

    _attend_causal(t, blk, lambda q0: q_ref[q0:q0 + blk, :], kcat_ref, v_ref, store)


def _mla_prompt(q, kvu, kr, n_batch, t, blk):
    tp = n_batch * t
    return pl.pallas_call(
        functools.partial(_mla_prompt_kernel, t, blk),
        grid=(n_batch, MLA_HEADS),
        in_specs=[
            pl.BlockSpec((t, Q_HEAD_COLS), lambda b, h: (b, h)),
            pl.BlockSpec((t, NOPE_DIM), lambda b, h: (b, 2 * h)),
            pl.BlockSpec((t, MLA_V_DIM), lambda b, h: (b, 2 * h + 1)),
            pl.BlockSpec((t, LANES), lambda b, h: (b, 0)),
        ],
        out_specs=pl.BlockSpec((t, MLA_V_DIM), lambda b, h: (b, h)),
        out_shape=jax.ShapeDtypeStruct((tp, MLA_HEADS * MLA_V_DIM), BF16),
        scratch_shapes=[pltpu.VMEM((t, 2 * NOPE_DIM), BF16)],
        compiler_params=_cparams(("parallel", "parallel")), name="mla_prompt",
    )(q, kvu, kvu, kr)


def _cumsum_kernel(t, lf_ref, tri_ref, c_ref):
    carry = jnp.zeros((1, LANES), F32)
    tri = tri_ref[...]
    for c in range(t // CHUNK):
        x = lf_ref[c * CHUNK:(c + 1) * CHUNK, :]
        x1 = x.astype(BF16)
        r1 = x - x1.astype(F32)
        x2 = r1.astype(BF16)
        x3 = (r1 - x2.astype(F32)).astype(BF16)
        cs = _dot(tri, x1) + _dot(tri, x2) + _dot(tri, x3) + carry
        c_ref[c * CHUNK:(c + 1) * CHUNK, :] = cs
        carry = cs[CHUNK - 1:CHUNK, :]


def _fox_cumsum(lf, tri, n_batch, t):
    return pl.pallas_call(
        functools.partial(_cumsum_kernel, t),
        grid=(n_batch,),
        in_specs=[pl.BlockSpec((t, LANES), lambda b: (b, 0)), _resident((CHUNK, CHUNK))],
        out_specs=pl.BlockSpec((t, LANES), lambda b: (b, 0)),
        out_shape=jax.ShapeDtypeStruct((n_batch * t, LANES), F32),
        compiler_params=_cparams(("parallel",)), name="fox_cumsum",
    )(lf, tri)


def _fox_prompt_kernel(t, blk, q_ref, k_ref, v_ref, cq_ref, ck_ref, o_ref):
    for i in range(2):
        def store(q0, o, i=i):
            if i == 0:
                o_ref[q0:q0 + blk, :] = o.astype(BF16)
            else:
                prev = o_ref[q0:q0 + blk, :]
                o_ref[q0:q0 + blk, :] = prev + pltpu.roll(o, LANES // 2, 1).astype(BF16)

        _attend_causal(t, blk, lambda q0, i=i: q_ref[i, q0:q0 + blk, :], k_ref.at[0], v_ref.at[0], store,
                       cq_ref=cq_ref.at[0, i], ck_ref=ck_ref.at[0, i])


def _fox_prompt(fq3, fk3, fv3, c_col, c_row, n_batch, t, blk):
    tp = n_batch * t
    pairs_per_kv = FOX_GROUP // 2
    return pl.pallas_call(
        functools.partial(_fox_prompt_kernel, t, blk),
        grid=(n_batch, FOX_HEADS // 2),
        in_specs=[
            pl.BlockSpec((2, t, LANES), lambda b, j: (j, b, 0)),
            pl.BlockSpec((1, t, LANES), lambda b, j: (j // pairs_per_kv, b, 0)),
            pl.BlockSpec((1, t, LANES), lambda b, j: (j // pairs_per_kv, b, 0)),
            pl.BlockSpec((1, 2, t, 1), lambda b, j: (b, j, 0, 0)),
            pl.BlockSpec((1, 2, 1, t), lambda b, j: (b, j, 0, 0)),
        ],
        out_specs=pl.BlockSpec((t, LANES), lambda b, j: (b, j)),
        out_shape=jax.ShapeDtypeStruct((tp, FOX_WIDTH), BF16),
        compiler_params=_cparams(("parallel", "parallel")), name="fox_prompt",
    )(fq3, fk3, fv3, c_col, c_row)


def _heads_mm_kernel(n_heads, kh, nh, x_ref, w_ref, o_ref):
    for h in range(n_heads):
        o_ref[:, h * nh:(h + 1) * nh] = _dot(x_ref[:, h * kh:(h + 1) * kh].astype(BF16), w_ref[h]).astype(o_ref.dtype)


def _heads_mm(x, w3, out_dtype, name):
    m = x.shape[0]
    n_heads, kh, nh = w3.shape
    return pl.pallas_call(
        functools.partial(_heads_mm_kernel, n_heads, kh, nh),
        in_specs=[pl.BlockSpec(memory_space=pltpu.VMEM), pl.BlockSpec(memory_space=pltpu.VMEM)],
        out_specs=pl.BlockSpec(memory_space=pltpu.VMEM),
        out_shape=jax.ShapeDtypeStruct((m, n_heads * nh), out_dtype),
        compiler_params=pltpu.CompilerParams(vmem_limit_bytes=VMEM_LIMIT), name=name,
    )(x, w3)


def _page_copies(pt_ref, bb, slot, n_pages, page_size, layer, specs, sems):
    copies = []
    for p in range(n_pages):
        page = pt_ref[bb, p]
        for k, (hbm, buf, on_rows) in enumerate(specs):
            if on_rows:
                dst = buf.at[slot, pl.ds(p * page_size, page_size), :]
            else:
                dst = buf.at[slot, :, pl.ds(p * page_size, page_size)]
            copies.append(pltpu.make_async_copy(hbm.at[layer, page], dst, sems.at[slot, k]))
    return copies


def _prefetch_pages(pt_ref, n_pages, page_size, layer, specs, sems):
    b = pl.program_id(0)
    nb = pl.num_programs(0)

    @pl.when(b == 0)
    def _():
        for c in _page_copies(pt_ref, 0, 0, n_pages, page_size, layer, specs, sems):
            c.start()

    @pl.when(b + 1 < nb)
    def _():
        for c in _page_copies(pt_ref, b + 1, (b + 1) % 2, n_pages, page_size, layer, specs, sems):
            c.start()

    slot = b % 2
    for c in _page_copies(pt_ref, b, slot, n_pages, page_size, layer, specs, sems):
        c.wait()
    return slot


def _mla_decode_kernel(layer, n_pages, page_size, pt_ref,
                       qa_ref, qr_ref, cnew_ref, krnew_ref, ckv_hbm, krt_hbm,
                       o_ref, ckv_buf, kr_buf, ckv_bf, sems):
    slot = _prefetch_pages(pt_ref, n_pages, page_size, layer,
                           [(ckv_hbm, ckv_buf, True), (krt_hbm, kr_buf, False)], sems)
    ckv_bf[...] = ckv_buf[slot].astype(BF16)
    qa = qa_ref[0]
    qr = qr_ref[0]
    s = _dot_nt(qa, ckv_bf[...]) + _dot(qr[:, 0:ROPE_DIM], kr_buf[slot].astype(BF16))
    cn = cnew_ref[0].astype(BF16).astype(F32)
    kn = krnew_ref[0].astype(BF16).astype(F32)
    s_new = (jnp.sum(qa.astype(F32) * cn, axis=1, keepdims=True)
             + jnp.sum(qr.astype(F32) * kn, axis=1, keepdims=True))
    m = jnp.maximum(jnp.max(s, axis=1, keepdims=True), s_new)
    p = jnp.exp(s - m)
    p_new = jnp.exp(s_new - m)
    l = jnp.sum(p, axis=1, keepdims=True) + p_new
    o = _dot(p.astype(BF16), ckv_bf[...]) + p_new.astype(BF16).astype(F32) * cn
    o_ref[0] = o / l


def _mla_decode(layer, page_table, qa3, qr3, cnew3, krnew3, cache_kv, cache_krt):
    db, n_pages = page_table.shape
    page_size = cache_kv.shape[2]
    past = n_pages * page_size
    per_b = lambda *tail: pl.BlockSpec((1,) + tail, lambda b, pt: (b,) + (0,) * len(tail))
    grid_spec = pltpu.PrefetchScalarGridSpec(
        num_scalar_prefetch=1, grid=(db,),
        in_specs=[per_b(MLA_HEADS, KV_LORA), per_b(MLA_HEADS, LANES), per_b(1, KV_LORA), per_b(1, LANES),
                  pl.BlockSpec(memory_space=pl.ANY), pl.BlockSpec(memory_space=pl.ANY)],
        out_specs=per_b(MLA_HEADS, KV_LORA),
        scratch_shapes=[pltpu.VMEM((2, past, KV_LORA), F32), pltpu.VMEM((2, ROPE_DIM, past), F32),
                        pltpu.VMEM((past, KV_LORA), BF16), pltpu.SemaphoreType.DMA((2, 2))],
    )
    return pl.pallas_call(
        functools.partial(_mla_decode_kernel, layer, n_pages, page_size),
        grid_spec=grid_spec,
        out_shape=jax.ShapeDtypeStruct((db, MLA_HEADS, KV_LORA), F32),
        compiler_params=_cparams(("arbitrary",)), name="mla_decode",
    )(page_table, qa3, qr3, cnew3, krnew3, cache_kv, cache_krt)


def _suffix_sum_lanes(x):
    n = x.shape[1]
    lane = lax.broadcasted_iota(I32, x.shape, 1)
    sh = 1
    while sh < n:
        shifted = pltpu.roll(x, n - sh, 1)
        x = x + jnp.where(lane < n - sh, shifted, 0.0)
        sh *= 2
    return x


def _fox_decode_kernel(layer, n_pages, page_size, pt_ref,
                       q_ref, knew_ref, vnew_ref, lfnew_ref, kt_hbm, vt_hbm, lft_hbm,
                       o_ref, k_buf, v_buf, lf_buf, sems):
    slot = _prefetch_pages(pt_ref, n_pages, page_size, layer,
                           [(kt_hbm, k_buf, False), (vt_hbm, v_buf, False), (lft_hbm, lf_buf, False)], sems)
    q = q_ref[0]
    lf = lf_buf[slot]
    bias = _suffix_sum_lanes(lf) - lf + lfnew_ref[0]
    s = _dot(q, k_buf[slot].astype(BF16)) + bias
    kn = knew_ref[0].astype(BF16).astype(F32)
    vn = vnew_ref[0].astype(BF16).astype(F32)
    s_new = jnp.sum(q.astype(F32) * kn, axis=1, keepdims=True)
    m = jnp.maximum(jnp.max(s, axis=1, keepdims=True), s_new)
    p = jnp.exp(s - m)
    p_new = jnp.exp(s_new - m)
    l = jnp.sum(p, axis=1, keepdims=True) + p_new
    o = _dot_nt(p.astype(BF16), v_buf[slot].astype(BF16)) + p_new.astype(BF16).astype(F32) * vn
    o_ref[0] = o / l


def _fox_decode(layer, page_table, qbd3, knew3, vnew3, lfnew3, cache_kt, cache_vt, cache_lft):
    db, n_pages = page_table.shape
    page_size = cache_kt.shape[3]
    past = n_pages * page_size
    per_b = lambda *tail: pl.BlockSpec((1,) + tail, lambda b, pt: (b,) + (0,) * len(tail))
    grid_spec = pltpu.PrefetchScalarGridSpec(
        num_scalar_prefetch=1, grid=(db,),
        in_specs=[per_b(FOX_HEADS, LANES), per_b(1, LANES), per_b(1, LANES), per_b(FOX_HEADS, 1),
                  pl.BlockSpec(memory_space=pl.ANY), pl.BlockSpec(memory_space=pl.ANY),
                  pl.BlockSpec(memory_space=pl.ANY)],
        out_specs=per_b(FOX_HEADS, LANES),
        scratch_shapes=[pltpu.VMEM((2, FOX_KV_WIDTH, past), F32), pltpu.VMEM((2, FOX_KV_WIDTH, past), F32),
                        pltpu.VMEM((2, FOX_HEADS, past), F32), pltpu.SemaphoreType.DMA((2, 3))],
    )
    return pl.pallas_call(
        functools.partial(_fox_decode_kernel, layer, n_pages, page_size),
        grid_spec=grid_spec,
        out_shape=jax.ShapeDtypeStruct((db, FOX_HEADS, LANES), F32),
        compiler_params=_cparams(("arbitrary",)), name="fox_decode",
    )(page_table, qbd3, knew3, vnew3, lfnew3, cache_kt, cache_vt, cache_lft)


def _group_top2(sel, sc):
    idx = lax.broadcasted_iota(I32, sel.shape, 0).astype(F32)
    big = float(EXPERTS_PER_GROUP)
    m1 = jnp.max(sel, axis=0, keepdims=True)
    i1 = jnp.min(jnp.where(sel == m1, idx, big), axis=0, keepdims=True)
    rest = jnp.where(idx == i1, -jnp.inf, sel)
    m2 = jnp.max(rest, axis=0, keepdims=True)
    i2 = jnp.min(jnp.where(rest == m2, idx, big), axis=0, keepdims=True)
    s1 = jnp.sum(jnp.where(idx == i1, sc, 0.0), axis=0, keepdims=True)
    s2 = jnp.sum(jnp.where(idx == i2, sc, 0.0), axis=0, keepdims=True)
    return m1 + m2, i1, i2, s1, s2


def _router_logits(rwt, x1):
    x_hi = x1.astype(BF16)
    x_lo = (x1 - x_hi.astype(F32)).astype(BF16)
    w_hi = rwt.astype(BF16)
    w_lo = (rwt - w_hi.astype(F32)).astype(BF16)
    return _dot_nt(w_hi, x_hi) + _dot_nt(w_hi, x_lo) + _dot_nt(w_lo, x_hi)


def _outproj_kernel(alpha, tm, x_ref, om_ref, og_ref, of_ref, wo_ref, g_ref, b_ref, rwt_ref, rb_ref, tri_ref,
                    x1_ref, x1r_ref, route_ref, cnt_ref, carry_ref):
    k1 = om_ref.shape[1]
    k2 = k1 + og_ref.shape[1]
    acc = (_dot(om_ref[...], wo_ref[0:k1, :]) + _dot(og_ref[...], wo_ref[k1:k2, :])
           + _dot(of_ref[...], wo_ref[k2:, :]))
    x1 = _layer_norm(alpha * x_ref[...] + acc, g_ref[...], b_ref[...])
    x1_ref[...] = x1
    x1r_ref[...] = x1.astype(BF16).reshape(x1r_ref.shape)

    sc = _sigmoid(_router_logits(rwt_ref[...], x1))
    sel = sc + rb_ref[...]
    best = None
    for g in range(N_GROUPS):
        r0 = g * EXPERTS_PER_GROUP
        gs, i1, i2, s1, s2 = _group_top2(sel[r0:r0 + EXPERTS_PER_GROUP], sc[r0:r0 + EXPERTS_PER_GROUP])
        cand = (gs, i1 + float(r0), i2 + float(r0), s1, s2)
        if best is None:
            best = cand
        else:
            better = gs > best[0]
            best = tuple(jnp.where(better, c, o) for c, o in zip(cand, best))
    _, e1, e2, s1, s2 = best
    denom = s1 + s2
    eidx = lax.broadcasted_iota(I32, (N_EXPERTS, tm), 0).astype(F32)
    hit1 = eidx == e1
    hit2 = eidx == e2
    onehot = jnp.where(hit1 | hit2, 1.0, 0.0)

    @pl.when(pl.program_id(0) == 0)
    def _():
        carry_ref[...] = jnp.zeros_like(carry_ref)

    before = _dot(onehot.astype(BF16), tri_ref[...]) + carry_ref[:, 0:1]
    r1 = jnp.sum(jnp.where(hit1, before, 0.0), axis=0, keepdims=True)
    r2 = jnp.sum(jnp.where(hit2, before, 0.0), axis=0, keepdims=True)
    carry_ref[...] = carry_ref[...] + jnp.sum(onehot, axis=1, keepdims=True)
    cnt_ref[...] = carry_ref[...]
    zero = jnp.zeros_like(e1)
    route_ref[...] = jnp.concatenate([e1, e2, s1 / denom, s2 / denom, r1, r2, zero, zero], axis=0)


def _outproj(alpha, h, o_mla, o_gm, o_fox, lw, rwt, rb, tri, tm):
    tt, d = h.shape
    row = lambda w, dt=None: pl.BlockSpec((tm, w), lambda i: (i, 0))
    return pl.pallas_call(
        functools.partial(_outproj_kernel, alpha, tm),
        grid=(tt // tm,),
        in_specs=[row(d), row(o_mla.shape[1]), row(o_gm.shape[1]), row(o_fox.shape[1]),
                  _resident(lw["w_out"].shape), _resident((1, d)), _resident((1, d)),
                  _resident((N_EXPERTS, d)), _resident((N_EXPERTS, 1)), _resident((tm, tm))],
        out_specs=(row(d), pl.BlockSpec((tm, d // LANES, LANES), lambda i: (i, 0, 0)),
                   pl.BlockSpec((SUBLANES, tm), lambda i: (0, i)),
                   pl.BlockSpec((N_EXPERTS, LANES), lambda i: (0, 0))),
        out_shape=(jax.ShapeDtypeStruct((tt, d), F32), jax.ShapeDtypeStruct((tt, d // LANES, LANES), BF16),
                   jax.ShapeDtypeStruct((SUBLANES, tt), F32), jax.ShapeDtypeStruct((N_EXPERTS, LANES), F32)),
        scratch_shapes=[pltpu.VMEM((N_EXPERTS, LANES), F32)],
        compiler_params=_cparams(("arbitrary",)), name="outproj_router",
    )(h, o_mla, o_gm, o_fox, lw["w_out"], lw["ln1_g"], lw["ln1_b"], rwt, rb, tri)


def _rowgather_kernel(rows, src_ref, nvalid_ref, x_hbm, o_ref, sem):
    base = pl.program_id(0) * rows

    def row_copy(s, r):
        return pltpu.make_async_copy(x_hbm.at[pl.ds(s, 1)], o_ref.at[pl.ds(r, 1)], sem)

    @pl.when(base < nvalid_ref[0])
    def _():
        def issue(r, carry):
            row_copy(src_ref[base + r], r).start()
            return carry

        lax.fori_loop(0, rows, issue, 0, unroll=ROW_DMA_UNROLL)

        def drain(r, carry):
            row_copy(0, r).wait()
            return carry

        lax.fori_loop(0, rows, drain, 0, unroll=ROW_DMA_UNROLL)

    @pl.when(base >= nvalid_ref[0])
    def _():
        o_ref[...] = jnp.zeros_like(o_ref)


def _rowgather(src_idx, nvalid, x, rows):
    n = src_idx.shape[0]
    blk = (rows,) + x.shape[1:]
    zeros = (0,) * (x.ndim - 1)
    grid_spec = pltpu.PrefetchScalarGridSpec(
        num_scalar_prefetch=2, grid=(n // rows,),
        in_specs=[pl.BlockSpec(memory_space=pl.ANY)],
        out_specs=pl.BlockSpec(blk, lambda i, src, nv: (i,) + zeros),
        scratch_shapes=[pltpu.SemaphoreType.DMA(())],
    )
    return pl.pallas_call(
        functools.partial(_rowgather_kernel, rows),
        grid_spec=grid_spec,
        out_shape=jax.ShapeDtypeStruct((n,) + x.shape[1:], x.dtype),
        compiler_params=_cparams(("arbitrary",)), name="rowgather",
    )(src_idx, nvalid, x)


def _moe_kernel(te_ref, first_ref, nused_ref, x_ref, wg_ref, wu_ref, wd_ref, y_ref, wg_bf, wu_bf, wd_bf):
    i = pl.program_id(0)
    tile = x_ref.shape[0]

    @pl.when(first_ref[i] == 1)
    def _():
        wg_bf[...] = wg_ref[...].astype(BF16)
        wu_bf[...] = wu_ref[...].astype(BF16)
        wd_bf[...] = wd_ref[...].astype(BF16)

    @pl.when(i < nused_ref[0])
    def _():
        x = x_ref[...].reshape(tile, wg_bf.shape[0])
        gate = _dot(x, wg_bf[...])
        a = gate * _sigmoid(gate) * _dot(x, wu_bf[...])
        y_ref[...] = _dot(a.astype(BF16), wd_bf[...]).astype(BF16).reshape(y_ref.shape)

    @pl.when(i >= nused_ref[0])
    def _():
        y_ref[...] = jnp.zeros_like(y_ref)


def _moe_gemm(layer, te, first, nused, xs, w_gate, w_up, w_down, tile):
    p = xs.shape[0]
    d, f = w_gate.shape[-2:]
    nt = p // tile
    last = lambda i, nu: jnp.minimum(i, nu[0] - 1)
    grid_spec = pltpu.PrefetchScalarGridSpec(
        num_scalar_prefetch=3, grid=(nt,),
        in_specs=[
            pl.BlockSpec((tile,) + xs.shape[1:], lambda i, te, fi, nu: (last(i, nu), 0, 0)),
            pl.BlockSpec((None, None, d, f), lambda i, te, fi, nu: (layer, te[i], 0, 0)),
            pl.BlockSpec((None, None, d, f), lambda i, te, fi, nu: (layer, te[i], 0, 0)),
            pl.BlockSpec((None, None, f, d), lambda i, te, fi, nu: (layer, te[i], 0, 0)),
        ],
        out_specs=pl.BlockSpec((tile,) + xs.shape[1:], lambda i, te, fi, nu: (i, 0, 0)),
        scratch_shapes=[pltpu.VMEM((d, f), BF16), pltpu.VMEM((d, f), BF16), pltpu.VMEM((f, d), BF16)],
    )
    return pl.pallas_call(
        _moe_kernel, grid_spec=grid_spec,
        out_shape=jax.ShapeDtypeStruct(xs.shape, BF16),
        compiler_params=_cparams(("arbitrary",)), name="moe_gemm",
    )(te, first, nused, xs, w_gate, w_up, w_down)


def _moe(layer, x1r, route, counts, w_gate, w_up, w_down):
    tt = x1r.shape[0]
    tile = MOE_TILE
    tiles_per_gather = GATHER_ROWS // tile
    nt = -(-(-(-2 * tt // tile) + N_EXPERTS) // tiles_per_gather) * tiles_per_gather
    p = nt * tile
    e1 = route[0].astype(I32)
    e2 = route[1].astype(I32)
    cnt = counts[:, 0].astype(I32)
    padded = ((cnt + tile - 1) // tile) * tile
    ends = jnp.cumsum(padded)
    starts = ends - padded
    pos1 = starts[e1] + route[4].astype(I32)
    pos2 = starts[e2] + route[5].astype(I32)
    nused = (ends[-1] // tile).astype(I32)
    tile_id = jnp.minimum(jnp.arange(nt, dtype=I32), nused - 1)
    te = jnp.sum((ends[None, :] <= (tile_id * tile)[:, None]).astype(I32), axis=1)
    te = jnp.minimum(te, N_EXPERTS - 1)
    first = jnp.concatenate([jnp.ones((1,), I32), (te[1:] != te[:-1]).astype(I32)])
    tok = jnp.arange(tt, dtype=I32)
    src = jnp.concatenate([tok, tok])
    dst = jnp.concatenate([pos1, pos2])
    slot_token = jnp.zeros((p,), I32).at[dst].set(src)
    xs = _rowgather(slot_token, (nused * tile).reshape(1), x1r, GATHER_ROWS)
    ys = _moe_gemm(layer, te, first, nused.reshape(1), xs, w_gate, w_up, w_down, tile)
    back_rows = max(r for r in range(SUBLANES, GATHER_ROWS_CAP + 1, SUBLANES) if (2 * tt) % r == 0)
    back = _rowgather(dst, jnp.full((1,), 2 * tt, I32), ys, back_rows)
    return back.reshape((2, tt) + x1r.shape[1:])


def _finish_kernel(alpha, x1_ref, y_ref, gw_ref, p_ref, wpe_ref, wpg_ref, g2_ref, b2_ref, g3_ref, b3_ref,
                   h_ref, hb_ref):
    gw = gw_ref[...]
    y0 = y_ref[0].reshape(x1_ref.shape).astype(F32)
    y1 = y_ref[1].reshape(x1_ref.shape).astype(F32)
    moe = y0 * gw[:, 2:3] + y1 * gw[:, 3:4]
    x2 = _layer_norm(alpha * x1_ref[...] + moe, g2_ref[...], b2_ref[...])
    e = _dot(p_ref[...].astype(BF16), wpe_ref[...]) * _sigmoid(_dot(x2.astype(BF16), wpg_ref[...]))
    x3 = _layer_norm(alpha * x2 + e, g3_ref[...], b3_ref[...])
    h_ref[...] = x3
    hb_ref[...] = x3.astype(BF16)


def _finish(alpha, x1, y2, route_t, p_l, lw, tm):
    tt, d = x1.shape
    row = lambda w: pl.BlockSpec((tm, w), lambda i: (i, 0))
    return pl.pallas_call(
        functools.partial(_finish_kernel, alpha),
        grid=(tt // tm,),
        in_specs=[row(d), pl.BlockSpec((2, tm) + y2.shape[2:], lambda i: (0, i, 0, 0)), row(SUBLANES),
                  row(p_l.shape[1]),
                  _resident(lw["w_pe"].shape), _resident(lw["w_pg"].shape),
                  _resident((1, d)), _resident((1, d)), _resident((1, d)), _resident((1, d))],
        out_specs=(row(d), row(d)),
        out_shape=(jax.ShapeDtypeStruct((tt, d), F32), jax.ShapeDtypeStruct((tt, d), BF16)),
        compiler_params=_cparams(("parallel",)), name="finish",
    )(x1, y2, route_t, p_l, lw["w_pe"], lw["w_pg"], lw["ln2_g"], lw["ln2_b"], lw["ln3_g"], lw["ln3_b"])


def _pack_layer_weights(l, w_in, q_norm_g, kv_norm_g, w_uq, w_ukv, gmlp_ln_g, gmlp_ln_b, gmlp_ws, gmlp_bs, fox_bf,
                        w_out, ln1_g, ln1_b, ln2_g, ln2_b, ln3_g, ln3_b, w_pe, w_pg):
    d = w_in.shape[1]
    wi = w_in[l]
    splits = np.cumsum([Q_LORA, KV_LORA, ROPE_DIM, GM_WIDTH, GM_WIDTH, FOX_WIDTH, FOX_KV_WIDTH, FOX_KV_WIDTH])
    q_lat, kv_lat, k_r, u, v, fq, fk, fv, fl = jnp.split(wi, [int(s) for s in splits], axis=1)
    w_in_p = jnp.concatenate(
        [q_lat, kv_lat, k_r, k_r, u, v, fq, fk, fv, fl, jnp.zeros((d, LANES - FOX_HEADS), F32)], axis=1)
    uq = w_uq[l].reshape(Q_LORA, MLA_HEADS, NOPE_DIM + ROPE_DIM)
    w_uq_p = jnp.concatenate([uq, uq[:, :, NOPE_DIM:]], axis=2).reshape(Q_LORA, MLA_HEADS * Q_HEAD_COLS)
    ukv = w_ukv[l].reshape(KV_LORA, MLA_HEADS, NOPE_DIM + MLA_V_DIM)
    w_uk_t = jnp.transpose(ukv[:, :, :NOPE_DIM], (1, 2, 0))
    w_uv = jnp.transpose(ukv[:, :, NOPE_DIM:], (1, 0, 2))
    ws = gmlp_ws[l]
    tril = jnp.tril(jnp.ones((CHUNK, CHUNK), dtype=bool))
    gm_w = jnp.stack([jnp.where(tril, ws, 0.0),
                      ws[:, 0:1, 0:1] * jnp.eye(CHUNK, dtype=F32)[None]])
    bs = gmlp_bs[l]
    bias_t = jnp.repeat(bs.T, GM_CH, axis=1)
    bias_s = jnp.broadcast_to(jnp.repeat(bs[:, 0], GM_CH)[None, :], (CHUNK, GM_WIDTH))
    bf = jnp.concatenate([fox_bf[l], jnp.zeros((LANES - FOX_HEADS,), F32)])[None, :]
    r = lambda a: a[l][None, :]
    return dict(
        w_in=w_in_p.astype(BF16), q_norm_g=r(q_norm_g), kv_norm_g=r(kv_norm_g), w_uq=w_uq_p.astype(BF16),
        w_ukv=w_ukv[l].astype(BF16), w_uk_t=w_uk_t.astype(BF16), w_uv=w_uv.astype(BF16),
        gm_ln_g=r(gmlp_ln_g), gm_ln_b=r(gmlp_ln_b), gm_w=gm_w.astype(BF16), gm_bias=jnp.stack([bias_t, bias_s]),
        fox_bf=bf, w_out=w_out[l].astype(BF16), ln1_g=r(ln1_g), ln1_b=r(ln1_b), ln2_g=r(ln2_g), ln2_b=r(ln2_b),
        ln3_g=r(ln3_g), ln3_b=r(ln3_b), w_pe=w_pe[l].astype(BF16), w_pg=w_pg[l].astype(BF16))


def _row_tile(tt):
    for tm in range(ROW_TILE_CAP, 0, -CHUNK):
        if tt % tm == 0:
            return tm
    raise ValueError("stacked row count must be a multiple of the gMLP chunk")


def kernel(x_prompt, x_sample, cache_mla_kv, cache_mla_krope, cache_fox_k, cache_fox_v, cache_fox_logf, page_table, p_prompt, p_sample, w_in, q_norm_g, kv_norm_g, w_uq, w_ukv, gmlp_ln_g, gmlp_ln_b, gmlp_ws, gmlp_bs, fox_bf, w_out, ln1_g, ln1_b, ln2_g, ln2_b, ln3_g, ln3_b, router_w, router_b, w_e_gate, w_e_up, w_e_down, w_pe, w_pg):
    n_batch, t, d = x_prompt.shape
    db, dec_seq, _ = x_sample.shape
    depth = w_in.shape[0]
    n_pages = page_table.shape[1]
    page_size = cache_mla_kv.shape[2]
    past = n_pages * page_size
    assert dec_seq == 1 and t % CHUNK == 0 and db % CHUNK == 0
    tp = n_batch * t
    tt = tp + db
    tm = _row_tile(tt)
    tm_fin = tm // 2 if (tm // 2) % SUBLANES == 0 else tm
    blk = min(ATT_BLOCK, t)
    alpha = (2 * depth) ** 0.25

    pos = jnp.concatenate([jnp.tile(jnp.arange(t), n_batch), jnp.full((db,), past)]).astype(F32)
    half = ROPE_DIM // 2
    inv = ROPE_BASE ** (-jnp.arange(half, dtype=F32) / half)
    ang = pos[:, None] * inv[None, :]
    cos_t = jnp.tile(jnp.cos(ang), (1, LANES // half))
    sin_h = jnp.sin(ang)
    sin_t = jnp.tile(jnp.concatenate([-sin_h, sin_h], axis=1), (1, LANES // ROPE_DIM))

    cache_krt = jnp.transpose(cache_mla_krope, (0, 1, 3, 2))
    n_pool = cache_fox_k.shape[1]
    cache_kt = jnp.transpose(cache_fox_k, (0, 1, 3, 4, 2)).reshape(depth, n_pool, FOX_KV_WIDTH, page_size)
    cache_vt = jnp.transpose(cache_fox_v, (0, 1, 3, 4, 2)).reshape(depth, n_pool, FOX_KV_WIDTH, page_size)
    cache_lft = jnp.transpose(cache_fox_logf, (0, 1, 3, 2))

    tri_lower = jnp.tril(jnp.ones((CHUNK, CHUNK), F32)).astype(BF16)
    tri_before = jnp.triu(jnp.ones((tm, tm), F32), k=1).astype(BF16)
    rwt = router_w.T
    rb = router_b[:, None]

    h = jnp.concatenate([x_prompt.reshape(tp, d), x_sample.reshape(db, d)], axis=0)
    hb = h.astype(BF16)
    outs = [[] for _ in range(11)]
    for l in range(depth):
        lw = _pack_layer_weights(l, w_in, q_norm_g, kv_norm_g, w_uq, w_ukv, gmlp_ln_g, gmlp_ln_b, gmlp_ws, gmlp_bs,
                                 fox_bf, w_out, ln1_g, ln1_b, ln2_g, ln2_b, ln3_g, ln3_b, w_pe, w_pg)
        q, ckv, kr, kvu, gv, go, fq3, fk, fv, fk3, fv3, lf = _inproj(hb, lw, cos_t, sin_t, tp // CHUNK, tm)

        o_mla_p = _mla_prompt(q, kvu, kr, n_batch, t, blk)
        c = _fox_cumsum(lf, tri_lower, n_batch, t)[:, :FOX_HEADS].reshape(n_batch, t, FOX_HEADS)
        c_hm = jnp.transpose(c, (0, 2, 1))
        o_fox_p = _fox_prompt(fq3, fk3, fv3, c_hm[..., None], c_hm[:, :, None, :], n_batch, t, blk)

        q_s = q[tp:].reshape(db, MLA_HEADS, Q_HEAD_COLS)
        qa = _heads_mm(q_s[:, :, :NOPE_DIM].reshape(db, MLA_HEADS * NOPE_DIM), lw["w_uk_t"], BF16, "mla_absorb_q")
        o_lat = _mla_decode(l, page_table, qa.reshape(db, MLA_HEADS, KV_LORA), q_s[:, :, NOPE_DIM:],
                            ckv[tp:, None, :], kr[tp:, None, :], cache_mla_kv, cache_krt)
        o_mla_s = _heads_mm(o_lat.reshape(db, MLA_HEADS * KV_LORA), lw["w_uv"], BF16, "mla_absorb_v")
        fq_s = fq3[:, tp:, :FOX_HEAD_DIM]
        zeros_q = jnp.zeros_like(fq_s)
        kvh = (jnp.arange(FOX_HEADS) // FOX_GROUP)[:, None, None]
        qbd = jnp.concatenate([jnp.where(kvh == 0, fq_s, zeros_q), jnp.where(kvh == 1, fq_s, zeros_q)], axis=2)
        o8 = _fox_decode(l, page_table, jnp.transpose(qbd, (1, 0, 2)), fk[tp:, None, :], fv[tp:, None, :],
                         lf[tp:, :FOX_HEADS, None], cache_kt, cache_vt, cache_lft)
        o8 = o8.reshape(db, FOX_KV_HEADS, FOX_GROUP, FOX_KV_HEADS, FOX_HEAD_DIM)
        o_fox_s = jnp.stack([o8[:, n, :, n, :] for n in range(FOX_KV_HEADS)], axis=1).reshape(db, FOX_WIDTH)

        o_mla = jnp.concatenate([o_mla_p, o_mla_s], axis=0)
        o_fox = jnp.concatenate([o_fox_p, o_fox_s.astype(BF16)], axis=0)
        x1, x1r, route, counts = _outproj(alpha, h, o_mla, go, o_fox, lw, rwt, rb, tri_before, tm)
        y2 = _moe(l, x1r, route, counts, w_e_gate, w_e_up, w_e_down)
        p_l = jnp.concatenate([p_prompt[l].reshape(tp, -1), p_sample[l].reshape(db, -1)], axis=0)
        h, hb = _finish(alpha, x1, y2, route.T, p_l, lw, tm_fin)

        leaves = (ckv[:tp].reshape(n_batch, t, KV_LORA), kr[:tp, :ROPE_DIM].reshape(n_batch, t, ROPE_DIM),
                  fk[:tp].reshape(n_batch, t, FOX_KV_HEADS, FOX_HEAD_DIM),
                  fv[:tp].reshape(n_batch, t, FOX_KV_HEADS, FOX_HEAD_DIM),
                  lf[:tp, :FOX_HEADS].reshape(n_batch, t, FOX_HEADS),
                  ckv[tp:].reshape(db, 1, KV_LORA), kr[tp:, :ROPE_DIM].reshape(db, 1, ROPE_DIM),
                  fk[tp:].reshape(db, 1, FOX_KV_HEADS, FOX_HEAD_DIM), fv[tp:].reshape(db, 1, FOX_KV_HEADS, FOX_HEAD_DIM),
                  lf[tp:, :FOX_HEADS].reshape(db, 1, FOX_HEADS), gv[tp:].reshape(db, 1, GM_WIDTH))
        for acc, leaf in zip(outs, leaves):
            acc.append(leaf)
    return (h[:tp].reshape(n_batch, t, d), h[tp:].reshape(db, 1, d)) + tuple(jnp.stack(o) for o in outs)
```

```python
import functools

import jax
import jax.numpy as jnp
import numpy as np
from jax import lax
from jax.experimental import pallas as pl
from jax.experimental.pallas import tpu as pltpu

F32 = jnp.float32
BF16 = jnp.bfloat16
I32 = jnp.int32

MLA_HEADS = 8
Q_LORA = 512
KV_LORA = 256
NOPE_DIM = 128
ROPE_DIM = 64
MLA_V_DIM = 128
ROPE_BASE = 10000.0
MLA_SCALE = (NOPE_DIM + ROPE_DIM) ** -0.5
GM_GROUPS = 4
GM_CH = 128
GM_WIDTH = GM_GROUPS * GM_CH
CHUNK = 128
FOX_HEADS = 8
FOX_KV_HEADS = 2
FOX_GROUP = FOX_HEADS // FOX_KV_HEADS
FOX_HEAD_DIM = 64
FOX_WIDTH = FOX_HEADS * FOX_HEAD_DIM
FOX_KV_WIDTH = FOX_KV_HEADS * FOX_HEAD_DIM
FOX_SCALE = FOX_HEAD_DIM ** -0.5
N_EXPERTS = 32
N_GROUPS = 4
EXPERTS_PER_GROUP = N_EXPERTS // N_GROUPS
D_EXPERT = 512
LN_EPS = 1e-5
RMS_EPS = 1e-6
NEG_INF = -1e30

LANES = 128
SUBLANES = 8
VMEM_LIMIT = 56 * 1024 * 1024

ROW_TILE_CAP = 640
ATT_BLOCK = 256
MOE_TILE = 256
GATHER_ROWS = 1024
GATHER_ROWS_CAP = 1280
ROW_DMA_UNROLL = 8

C_QLAT = 0
C_KVLAT = C_QLAT + Q_LORA
C_KR = C_KVLAT + KV_LORA
C_U = C_KR + LANES
C_V = C_U + GM_WIDTH
C_FQ = C_V + GM_WIDTH
C_FK = C_FQ + FOX_WIDTH
C_FV = C_FK + FOX_KV_WIDTH
C_FL = C_FV + FOX_KV_WIDTH
IN_COLS_P = C_FL + LANES
Q_HEAD_COLS = 2 * LANES


def _cparams(sem, **kw):
    return pltpu.CompilerParams(dimension_semantics=sem, vmem_limit_bytes=VMEM_LIMIT, **kw)


def _dot(a, b):
    return jnp.dot(a, b, preferred_element_type=F32)


def _dot_nt(a, b):
    return lax.dot_general(a, b, (((1,), (1,)), ((), ())), preferred_element_type=F32)


def _layer_norm(x, g, b):
    mu = jnp.mean(x, axis=-1, keepdims=True)
    xc = x - mu
    var = jnp.mean(xc * xc, axis=-1, keepdims=True)
    return xc * lax.rsqrt(var + LN_EPS) * g + b


def _rms_norm(x, g):
    return x * lax.rsqrt(jnp.mean(x * x, axis=-1, keepdims=True) + RMS_EPS) * g


def _gelu_tanh(x):
    return 0.5 * x * (1.0 + jnp.tanh(0.7978845608028654 * (x + 0.044715 * (x * x * x))))


def _log_sigmoid(x):
    return jnp.minimum(x, 0.0) - jnp.log1p(jnp.exp(-jnp.abs(x)))


def _sigmoid(x):
    return 1.0 / (1.0 + jnp.exp(-x))


def _resident(shape):
    nd = len(shape)
    return pl.BlockSpec(shape, lambda *_: (0,) * nd, pipeline_mode=pl.Buffered(1))


def _layer_resident(layer, shape):
    nd = len(shape)
    return pl.BlockSpec((None,) + tuple(shape), lambda *_: (layer,) + (0,) * nd, pipeline_mode=pl.Buffered(1))


def _inproj_kernel(n_prompt_chunks, tm,
                   hb_ref, win_ref, qg_ref, kvg_ref, wuq_ref, wukv_ref, cos_ref, sin_ref,
                   gg_ref, gb_ref, bf_ref, gw_ref, gbias_ref,
                   q_ref, ckv_ref, kr_ref, kvu_ref, gv_ref, go_ref, fq3_ref, fk_ref, fv_ref,
                   fk3_ref, fv3_ref, lf_ref):
    hb = hb_ref[...]

    def z(c0, width):
        return _dot(hb, win_ref[:, c0:c0 + width])

    cos = cos_ref[...]
    sin = sin_ref[...]
    lane = lax.broadcasted_iota(I32, (tm, LANES), 1)
    low_half = lane < (LANES // 2)

    def rope(blk):
        return blk * cos + pltpu.roll(blk, ROPE_DIM // 2, 1) * sin

    qn = _rms_norm(z(C_QLAT, Q_LORA), qg_ref[...])
    qf = _dot(qn.astype(BF16), wuq_ref[...])
    for h in range(MLA_HEADS):
        c0 = h * Q_HEAD_COLS
        q_ref[:, c0:c0 + LANES] = (qf[:, c0:c0 + LANES] * MLA_SCALE).astype(BF16)
        q_ref[:, c0 + LANES:c0 + 2 * LANES] = (rope(qf[:, c0 + LANES:c0 + 2 * LANES]) * MLA_SCALE).astype(BF16)

    ckv = _rms_norm(z(C_KVLAT, KV_LORA), kvg_ref[...])
    ckv_ref[...] = ckv
    kvu_ref[...] = _dot(ckv.astype(BF16), wukv_ref[...]).astype(BF16)
    kr_ref[...] = jnp.where(low_half, rope(z(C_KR, LANES)), 0.0)

    u = _gelu_tanh(z(C_U, GM_WIDTH))
    v = _layer_norm(_gelu_tanh(z(C_V, GM_WIDTH)), gg_ref[...], gb_ref[...])
    gv_ref[...] = v
    nch = tm // CHUNK
    for c in range(nch):
        is_sample = (pl.program_id(0) * nch + c) >= n_prompt_chunks
        r0 = c * CHUNK
        for g in range(GM_GROUPS):
            g0 = g * GM_CH
            w = jnp.where(is_sample, gw_ref[1, g], gw_ref[0, g])
            bias = jnp.where(is_sample, gbias_ref[1, :, g0:g0 + GM_CH], gbias_ref[0, :, g0:g0 + GM_CH])
            s = _dot(w, v[r0:r0 + CHUNK, g0:g0 + GM_CH].astype(BF16)) + bias
            go_ref[r0:r0 + CHUNK, g0:g0 + GM_CH] = (u[r0:r0 + CHUNK, g0:g0 + GM_CH] * s).astype(BF16)

    for j in range(FOX_HEADS // 2):
        pair = z(C_FQ + j * LANES, LANES) * FOX_SCALE
        fq3_ref[2 * j] = pair.astype(BF16)
        fq3_ref[2 * j + 1] = pltpu.roll(pair, LANES // 2, 1).astype(BF16)
    fk = z(C_FK, LANES)
    fv = z(C_FV, LANES)
    fk_ref[...] = fk
    fv_ref[...] = fv
    fk3_ref[0] = jnp.where(low_half, fk, 0.0).astype(BF16)
    fk3_ref[1] = jnp.where(low_half, pltpu.roll(fk, LANES // 2, 1), 0.0).astype(BF16)
    fv3_ref[0] = jnp.where(low_half, fv, 0.0).astype(BF16)
    fv3_ref[1] = jnp.where(low_half, pltpu.roll(fv, LANES // 2, 1), 0.0).astype(BF16)
    lf_ref[...] = _log_sigmoid(z(C_FL, LANES) + bf_ref[...])


def _inproj(layer, hb, pw, cos_t, sin_t, n_prompt_chunks, tm):
    tt, d = hb.shape
    lres = functools.partial(_layer_resident, layer)
    row = lambda w: pl.BlockSpec((tm, w), lambda i: (i, 0))
    out_shapes = (
        jax.ShapeDtypeStruct((tt, MLA_HEADS * Q_HEAD_COLS), BF16),
        jax.ShapeDtypeStruct((tt, KV_LORA), F32),
        jax.ShapeDtypeStruct((tt, LANES), F32),
        jax.ShapeDtypeStruct((tt, MLA_HEADS * (NOPE_DIM + MLA_V_DIM)), BF16),
        jax.ShapeDtypeStruct((tt, GM_WIDTH), F32),
        jax.ShapeDtypeStruct((tt, GM_WIDTH), BF16),
        jax.ShapeDtypeStruct((FOX_HEADS, tt, LANES), BF16),
        jax.ShapeDtypeStruct((tt, LANES), F32),
        jax.ShapeDtypeStruct((tt, LANES), F32),
        jax.ShapeDtypeStruct((FOX_KV_HEADS, tt, LANES), BF16),
        jax.ShapeDtypeStruct((FOX_KV_HEADS, tt, LANES), BF16),
        jax.ShapeDtypeStruct((tt, LANES), F32),
    )
    out_specs = (
        row(MLA_HEADS * Q_HEAD_COLS), row(KV_LORA), row(LANES), row(MLA_HEADS * (NOPE_DIM + MLA_V_DIM)),
        row(GM_WIDTH), row(GM_WIDTH),
        pl.BlockSpec((FOX_HEADS, tm, LANES), lambda i: (0, i, 0)),
        row(LANES), row(LANES),
        pl.BlockSpec((FOX_KV_HEADS, tm, LANES), lambda i: (0, i, 0)),
        pl.BlockSpec((FOX_KV_HEADS, tm, LANES), lambda i: (0, i, 0)),
        row(LANES),
    )
    in_specs = [
        row(d), lres((d, IN_COLS_P)), lres((1, Q_LORA)), lres((1, KV_LORA)),
        lres((Q_LORA, MLA_HEADS * Q_HEAD_COLS)), lres((KV_LORA, MLA_HEADS * (NOPE_DIM + MLA_V_DIM))),
        row(LANES), row(LANES),
        lres((1, GM_WIDTH)), lres((1, GM_WIDTH)), lres((1, LANES)),
        lres((2, GM_GROUPS, CHUNK, CHUNK)), lres((2, CHUNK, GM_WIDTH)),
    ]
    return pl.pallas_call(
        functools.partial(_inproj_kernel, n_prompt_chunks, tm),
        grid=(tt // tm,), in_specs=in_specs, out_specs=out_specs, out_shape=out_shapes,
        compiler_params=_cparams(("parallel",)), name="inproj",
    )(hb, pw["w_in"], pw["q_norm_g"], pw["kv_norm_g"], pw["w_uq"], pw["w_ukv"], cos_t, sin_t,
      pw["gm_ln_g"], pw["gm_ln_b"], pw["fox_bf"], pw["gm_w"], pw["gm_bias"])


def _attend_causal(t, blk, load_q, k_ref, v_ref, store_o, cq_ref=None, ck_ref=None):
    row = lax.broadcasted_iota(I32, (blk, blk), 0)
    col = lax.broadcasted_iota(I32, (blk, blk), 1)
    causal = col <= row
    for qi in range(t // blk):
        q0 = qi * blk
        e = q0 + blk
        s = _dot_nt(load_q(q0), k_ref[0:e, :])
        if cq_ref is not None:
            s = s + cq_ref[q0:e, :] - ck_ref[:, 0:e]
        s_d = jnp.where(causal, s[:, q0:e], NEG_INF)
        m = jnp.max(s_d, axis=1, keepdims=True)
        if qi > 0:
            s_o = s[:, 0:q0]
            m = jnp.maximum(m, jnp.max(s_o, axis=1, keepdims=True))
            p_o = jnp.exp(s_o - m)
        p_d = jnp.exp(s_d - m)
        l = jnp.sum(p_d, axis=1, keepdims=True)
        o = _dot(p_d.astype(BF16), v_ref[q0:e, :])
        if qi > 0:
            l = l + jnp.sum(p_o, axis=1, keepdims=True)
            o = o + _dot(p_o.astype(BF16), v_ref[0:q0, :])
        store_o(q0, o / l)


def _mla_prompt_kernel(t, blk, q_ref, kn_ref, v_ref, kr_ref, o_ref, kcat_ref):
    kcat_ref[:, 0:NOPE_DIM] = kn_ref[...]
    kcat_ref[:, NOPE_DIM:2 * NOPE_DIM] = kr_ref[...].astype(BF16)

    def store(q0, o):
        o_ref[q0:q0 + blk, :] = o.astype(BF16)

    _attend_causal(t, blk, lambda q0: q_ref[q0:q0 + blk, :], kcat_ref, v_ref, store)


def _mla_prompt(q, kvu, kr, n_batch, t, blk):
    tp = n_batch * t
    return pl.pallas_call(
        functools.partial(_mla_prompt_kernel, t, blk),
        grid=(n_batch, MLA_HEADS),
        in_specs=[
            pl.BlockSpec((t, Q_HEAD_COLS), lambda b, h: (b, h)),
            pl.BlockSpec((t, NOPE_DIM), lambda b, h: (b, 2 * h)),
            pl.BlockSpec((t, MLA_V_DIM), lambda b, h: (b, 2 * h + 1)),
            pl.BlockSpec((t, LANES), lambda b, h: (b, 0)),
        ],
        out_specs=pl.BlockSpec((t, MLA_V_DIM), lambda b, h: (b, h)),
        out_shape=jax.ShapeDtypeStruct((tp, MLA_HEADS * MLA_V_DIM), BF16),
        scratch_shapes=[pltpu.VMEM((t, 2 * NOPE_DIM), BF16)],
        compiler_params=_cparams(("parallel", "parallel")), name="mla_prompt",
    )(q, kvu, kvu, kr)


def _cumsum_kernel(t, lf_ref, tri_ref, c_ref):
    carry = jnp.zeros((1, LANES), F32)
    tri = tri_ref[...]
    for c in range(t // CHUNK):
        x = lf_ref[c * CHUNK:(c + 1) * CHUNK, :]
        x1 = x.astype(BF16)
        r1 = x - x1.astype(F32)
        x2 = r1.astype(BF16)
        x3 = (r1 - x2.astype(F32)).astype(BF16)
        cs = _dot(tri, x1) + _dot(tri, x2) + _dot(tri, x3) + carry
        c_ref[c * CHUNK:(c + 1) * CHUNK, :] = cs
        carry = cs[CHUNK - 1:CHUNK, :]


def _fox_cumsum(lf, tri, n_batch, t):
    return pl.pallas_call(
        functools.partial(_cumsum_kernel, t),
        grid=(n_batch,),
        in_specs=[pl.BlockSpec((t, LANES), lambda b: (b, 0)), _resident((CHUNK, CHUNK))],
        out_specs=pl.BlockSpec((t, LANES), lambda b: (b, 0)),
        out_shape=jax.ShapeDtypeStruct((n_batch * t, LANES), F32),
        compiler_params=_cparams(("parallel",)), name="fox_cumsum",
    )(lf, tri)


def _fox_prompt_kernel(t, blk, q_ref, k_ref, v_ref, cq_ref, ck_ref, o_ref):
    for i in range(2):
        def store(q0, o, i=i):
            if i == 0:
                o_ref[q0:q0 + blk, :] = o.astype(BF16)
            else:
                prev = o_ref[q0:q0 + blk, :]
                o_ref[q0:q0 + blk, :] = prev + pltpu.roll(o, LANES // 2, 1).astype(BF16)

        _attend_causal(t, blk, lambda q0, i=i: q_ref[i, q0:q0 + blk, :], k_ref.at[0], v_ref.at[0], store,
                       cq_ref=cq_ref.at[0, i], ck_ref=ck_ref.at[0, i])


def _fox_prompt(fq3, fk3, fv3, c_col, c_row, n_batch, t, blk):
    tp = n_batch * t
    pairs_per_kv = FOX_GROUP // 2
    return pl.pallas_call(
        functools.partial(_fox_prompt_kernel, t, blk),
        grid=(n_batch, FOX_HEADS // 2),
        in_specs=[
            pl.BlockSpec((2, t, LANES), lambda b, j: (j, b, 0)),
            pl.BlockSpec((1, t, LANES), lambda b, j: (j // pairs_per_kv, b, 0)),
            pl.BlockSpec((1, t, LANES), lambda b, j: (j // pairs_per_kv, b, 0)),
            pl.BlockSpec((1, 2, t, 1), lambda b, j: (b, j, 0, 0)),
            pl.BlockSpec((1, 2, 1, t), lambda b, j: (b, j, 0, 0)),
        ],
        out_specs=pl.BlockSpec((t, LANES), lambda b, j: (b, j)),
        out_shape=jax.ShapeDtypeStruct((tp, FOX_WIDTH), BF16),
        compiler_params=_cparams(("parallel", "parallel")), name="fox_prompt",
    )(fq3, fk3, fv3, c_col, c_row)


def _heads_mm_kernel(n_heads, kh, nh, x_ref, w_ref, o_ref):
    for h in range(n_heads):
        o_ref[:, h * nh:(h + 1) * nh] = _dot(x_ref[:, h * kh:(h + 1) * kh].astype(BF16), w_ref[h]).astype(o_ref.dtype)


def _heads_mm(x, w3, out_dtype, name):
    m = x.shape[0]
    n_heads, kh, nh = w3.shape
    return pl.pallas_call(
        functools.partial(_heads_mm_kernel, n_heads, kh, nh),
        in_specs=[pl.BlockSpec(memory_space=pltpu.VMEM), pl.BlockSpec(memory_space=pltpu.VMEM)],
        out_specs=pl.BlockSpec(memory_space=pltpu.VMEM),
        out_shape=jax.ShapeDtypeStruct((m, n_heads * nh), out_dtype),
        compiler_params=pltpu.CompilerParams(vmem_limit_bytes=VMEM_LIMIT), name=name,
    )(x, w3)


def _page_copies(pt_ref, bb, slot, n_pages, page_size, layer, specs, sems):
    copies = []
    for p in range(n_pages):
        page = pt_ref[bb, p]
        for k, (hbm, buf, on_rows) in enumerate(specs):
            if on_rows:
                dst = buf.at[slot, pl.ds(p * page_size, page_size), :]
            else:
                dst = buf.at[slot, :, pl.ds(p * page_size, page_size)]
            copies.append(pltpu.make_async_copy(hbm.at[layer, page], dst, sems.at[slot, k]))
    return copies


def _prefetch_pages(pt_ref, n_pages, page_size, layer, specs, sems):
    b = pl.program_id(0)
    nb = pl.num_programs(0)

    @pl.when(b == 0)
    def _():
        for c in _page_copies(pt_ref, 0, 0, n_pages, page_size, layer, specs, sems):
            c.start()

    @pl.when(b + 1 < nb)
    def _():
        for c in _page_copies(pt_ref, b + 1, (b + 1) % 2, n_pages, page_size, layer, specs, sems):
            c.start()

    slot = b % 2
    for c in _page_copies(pt_ref, b, slot, n_pages, page_size, layer, specs, sems):
        c.wait()
    return slot


def _mla_decode_kernel(layer, n_pages, page_size, pt_ref,
                       qa_ref, qr_ref, cnew_ref, krnew_ref, ckv_hbm, krt_hbm,
                       o_ref, ckv_buf, kr_buf, ckv_bf, sems):
    slot = _prefetch_pages(pt_ref, n_pages, page_size, layer,
                           [(ckv_hbm, ckv_buf, True), (krt_hbm, kr_buf, False)], sems)
    ckv_bf[...] = ckv_buf[slot].astype(BF16)
    qa = qa_ref[0]
    qr = qr_ref[0]
    s = _dot_nt(qa, ckv_bf[...]) + _dot(qr[:, 0:ROPE_DIM], kr_buf[slot].astype(BF16))
    cn = cnew_ref[0].astype(BF16).astype(F32)
    kn = krnew_ref[0].astype(BF16).astype(F32)
    s_new = (jnp.sum(qa.astype(F32) * cn, axis=1, keepdims=True)
             + jnp.sum(qr.astype(F32) * kn, axis=1, keepdims=True))
    m = jnp.maximum(jnp.max(s, axis=1, keepdims=True), s_new)
    p = jnp.exp(s - m)
    p_new = jnp.exp(s_new - m)
    l = jnp.sum(p, axis=1, keepdims=True) + p_new
    o = _dot(p.astype(BF16), ckv_bf[...]) + p_new.astype(BF16).astype(F32) * cn
    o_ref[0] = o / l


def _mla_decode(layer, page_table, qa3, qr3, cnew3, krnew3, cache_kv, cache_krt):
    db, n_pages = page_table.shape
    page_size = cache_kv.shape[2]
    past = n_pages * page_size
    per_b = lambda *tail: pl.BlockSpec((1,) + tail, lambda b, pt: (b,) + (0,) * len(tail))
    grid_spec = pltpu.PrefetchScalarGridSpec(
        num_scalar_prefetch=1, grid=(db,),
        in_specs=[per_b(MLA_HEADS, KV_LORA), per_b(MLA_HEADS, LANES), per_b(1, KV_LORA), per_b(1, LANES),
                  pl.BlockSpec(memory_space=pl.ANY), pl.BlockSpec(memory_space=pl.ANY)],
        out_specs=per_b(MLA_HEADS, KV_LORA),
        scratch_shapes=[pltpu.VMEM((2, past, KV_LORA), F32), pltpu.VMEM((2, ROPE_DIM, past), F32),
                        pltpu.VMEM((past, KV_LORA), BF16), pltpu.SemaphoreType.DMA((2, 2))],
    )
    return pl.pallas_call(
        functools.partial(_mla_decode_kernel, layer, n_pages, page_size),
        grid_spec=grid_spec,
        out_shape=jax.ShapeDtypeStruct((db, MLA_HEADS, KV_LORA), F32),
        compiler_params=_cparams(("arbitrary",)), name="mla_decode",
    )(page_table, qa3, qr3, cnew3, krnew3, cache_kv, cache_krt)


def _suffix_sum_lanes(x):
    n = x.shape[1]
    lane = lax.broadcasted_iota(I32, x.shape, 1)
    sh = 1
    while sh < n:
        shifted = pltpu.roll(x, n - sh, 1)
        x = x + jnp.where(lane < n - sh, shifted, 0.0)
        sh *= 2
    return x


def _fox_decode_kernel(layer, n_pages, page_size, pt_ref,
                       q_ref, knew_ref, vnew_ref, lfnew_ref, kt_hbm, vt_hbm, lft_hbm,
                       o_ref, k_buf, v_buf, lf_buf, sems):
    slot = _prefetch_pages(pt_ref, n_pages, page_size, layer,
                           [(kt_hbm, k_buf, False), (vt_hbm, v_buf, False), (lft_hbm, lf_buf, False)], sems)
    q = q_ref[0]
    lf = lf_buf[slot]
    bias = _suffix_sum_lanes(lf) - lf + lfnew_ref[0]
    s = _dot(q, k_buf[slot].astype(BF16)) + bias
    kn = knew_ref[0].astype(BF16).astype(F32)
    vn = vnew_ref[0].astype(BF16).astype(F32)
    s_new = jnp.sum(q.astype(F32) * kn, axis=1, keepdims=True)
    m = jnp.maximum(jnp.max(s, axis=1, keepdims=True), s_new)
    p = jnp.exp(s - m)
    p_new = jnp.exp(s_new - m)
    l = jnp.sum(p, axis=1, keepdims=True) + p_new
    o = _dot_nt(p.astype(BF16), v_buf[slot].astype(BF16)) + p_new.astype(BF16).astype(F32) * vn
    o_ref[0] = o / l


def _fox_decode(layer, page_table, qbd3, knew3, vnew3, lfnew3, cache_kt, cache_vt, cache_lft):
    db, n_pages = page_table.shape
    page_size = cache_kt.shape[3]
    past = n_pages * page_size
    per_b = lambda *tail: pl.BlockSpec((1,) + tail, lambda b, pt: (b,) + (0,) * len(tail))
    grid_spec = pltpu.PrefetchScalarGridSpec(
        num_scalar_prefetch=1, grid=(db,),
        in_specs=[per_b(FOX_HEADS, LANES), per_b(1, LANES), per_b(1, LANES), per_b(FOX_HEADS, 1),
                  pl.BlockSpec(memory_space=pl.ANY), pl.BlockSpec(memory_space=pl.ANY),
                  pl.BlockSpec(memory_space=pl.ANY)],
        out_specs=per_b(FOX_HEADS, LANES),
        scratch_shapes=[pltpu.VMEM((2, FOX_KV_WIDTH, past), F32), pltpu.VMEM((2, FOX_KV_WIDTH, past), F32),
                        pltpu.VMEM((2, FOX_HEADS, past), F32), pltpu.SemaphoreType.DMA((2, 3))],
    )
    return pl.pallas_call(
        functools.partial(_fox_decode_kernel, layer, n_pages, page_size),
        grid_spec=grid_spec,
        out_shape=jax.ShapeDtypeStruct((db, FOX_HEADS, LANES), F32),
        compiler_params=_cparams(("arbitrary",)), name="fox_decode",
    )(page_table, qbd3, knew3, vnew3, lfnew3, cache_kt, cache_vt, cache_lft)


def _group_top2(sel, sc):
    idx = lax.broadcasted_iota(I32, sel.shape, 0).astype(F32)
    big = float(EXPERTS_PER_GROUP)
    m1 = jnp.max(sel, axis=0, keepdims=True)
    i1 = jnp.min(jnp.where(sel == m1, idx, big), axis=0, keepdims=True)
    rest = jnp.where(idx == i1, -jnp.inf, sel)
    m2 = jnp.max(rest, axis=0, keepdims=True)
    i2 = jnp.min(jnp.where(rest == m2, idx, big), axis=0, keepdims=True)
    s1 = jnp.sum(jnp.where(idx == i1, sc, 0.0), axis=0, keepdims=True)
    s2 = jnp.sum(jnp.where(idx == i2, sc, 0.0), axis=0, keepdims=True)
    return m1 + m2, i1, i2, s1, s2


def _router_logits(rwt, x1):
    x_hi = x1.astype(BF16)
    x_lo = (x1 - x_hi.astype(F32)).astype(BF16)
    w_hi = rwt.astype(BF16)
    w_lo = (rwt - w_hi.astype(F32)).astype(BF16)
    return _dot_nt(w_hi, x_hi) + _dot_nt(w_hi, x_lo) + _dot_nt(w_lo, x_hi)


def _outproj_kernel(alpha, tm, x_ref, om_ref, og_ref, of_ref, wo_ref, g_ref, b_ref, rwt_ref, rb_ref, tri_ref,
                    x1_ref, x1r_ref, route_ref, cnt_ref, carry_ref):
    k1 = om_ref.shape[1]
    k2 = k1 + og_ref.shape[1]
    acc = (_dot(om_ref[...], wo_ref[0:k1, :]) + _dot(og_ref[...], wo_ref[k1:k2, :])
           + _dot(of_ref[...], wo_ref[k2:, :]))
    x1 = _layer_norm(alpha * x_ref[...] + acc, g_ref[...], b_ref[...])
    x1_ref[...] = x1
    x1r_ref[...] = x1.astype(BF16).reshape(x1r_ref.shape)

    sc = _sigmoid(_router_logits(rwt_ref[...], x1))
    sel = sc + rb_ref[...]
    best = None
    for g in range(N_GROUPS):
        r0 = g * EXPERTS_PER_GROUP
        gs, i1, i2, s1, s2 = _group_top2(sel[r0:r0 + EXPERTS_PER_GROUP], sc[r0:r0 + EXPERTS_PER_GROUP])
        cand = (gs, i1 + float(r0), i2 + float(r0), s1, s2)
        if best is None:
            best = cand
        else:
            better = gs > best[0]
            best = tuple(jnp.where(better, c, o) for c, o in zip(cand, best))
    _, e1, e2, s1, s2 = best
    denom = s1 + s2
    eidx = lax.broadcasted_iota(I32, (N_EXPERTS, tm), 0).astype(F32)
    hit1 = eidx == e1
    hit2 = eidx == e2
    onehot = jnp.where(hit1 | hit2, 1.0, 0.0)

    @pl.when(pl.program_id(0) == 0)
    def _():
        carry_ref[...] = jnp.zeros_like(carry_ref)

    before = _dot(onehot.astype(BF16), tri_ref[...]) + carry_ref[:, 0:1]
    r1 = jnp.sum(jnp.where(hit1, before, 0.0), axis=0, keepdims=True)
    r2 = jnp.sum(jnp.where(hit2, before, 0.0), axis=0, keepdims=True)
    carry_ref[...] = carry_ref[...] + jnp.sum(onehot, axis=1, keepdims=True)
    cnt_ref[...] = carry_ref[...]
    zero = jnp.zeros_like(e1)
    route_ref[...] = jnp.concatenate([e1, e2, s1 / denom, s2 / denom, r1, r2, zero, zero], axis=0)


def _outproj(alpha, layer, h, o_mla, o_gm, o_fox, pw, rwt, rb, tri, tm):
    tt, d = h.shape
    lres = functools.partial(_layer_resident, layer)
    row = lambda w, dt=None: pl.BlockSpec((tm, w), lambda i: (i, 0))
    return pl.pallas_call(
        functools.partial(_outproj_kernel, alpha, tm),
        grid=(tt // tm,),
        in_specs=[row(d), row(o_mla.shape[1]), row(o_gm.shape[1]), row(o_fox.shape[1]),
                  lres(pw["w_out"].shape[1:]), lres((1, d)), lres((1, d)),
                  _resident((N_EXPERTS, d)), _resident((N_EXPERTS, 1)), _resident((tm, tm))],
        out_specs=(row(d), pl.BlockSpec((tm, d // LANES, LANES), lambda i: (i, 0, 0)),
                   pl.BlockSpec((SUBLANES, tm), lambda i: (0, i)),
                   pl.BlockSpec((N_EXPERTS, LANES), lambda i: (0, 0))),
        out_shape=(jax.ShapeDtypeStruct((tt, d), F32), jax.ShapeDtypeStruct((tt, d // LANES, LANES), BF16),
                   jax.ShapeDtypeStruct((SUBLANES, tt), F32), jax.ShapeDtypeStruct((N_EXPERTS, LANES), F32)),
        scratch_shapes=[pltpu.VMEM((N_EXPERTS, LANES), F32)],
        compiler_params=_cparams(("arbitrary",)), name="outproj_router",
    )(h, o_mla, o_gm, o_fox, pw["w_out"], pw["ln1_g"], pw["ln1_b"], rwt, rb, tri)


def _rowgather_kernel(rows, src_ref, nvalid_ref, x_hbm, o_ref, sem):
    base = pl.program_id(0) * rows

    def row_copy(s, r):
        return pltpu.make_async_copy(x_hbm.at[pl.ds(s, 1)], o_ref.at[pl.ds(r, 1)], sem)

    @pl.when(base < nvalid_ref[0])
    def _():
        def issue(r, carry):
            row_copy(src_ref[base + r], r).start()
            return carry

        lax.fori_loop(0, rows, issue, 0, unroll=ROW_DMA_UNROLL)

        def drain(r, carry):
            row_copy(0, r).wait()
            return carry

        lax.fori_loop(0, rows, drain, 0, unroll=ROW_DMA_UNROLL)

    @pl.when(base >= nvalid_ref[0])
    def _():
        o_ref[...] = jnp.zeros_like(o_ref)


def _rowgather(src_idx, nvalid, x, rows):
    n = src_idx.shape[0]
    blk = (rows,) + x.shape[1:]
    zeros = (0,) * (x.ndim - 1)
    grid_spec = pltpu.PrefetchScalarGridSpec(
        num_scalar_prefetch=2, grid=(n // rows,),
        in_specs=[pl.BlockSpec(memory_space=pl.ANY)],
        out_specs=pl.BlockSpec(blk, lambda i, src, nv: (i,) + zeros),
        scratch_shapes=[pltpu.SemaphoreType.DMA(())],
    )
    return pl.pallas_call(
        functools.partial(_rowgather_kernel, rows),
        grid_spec=grid_spec,
        out_shape=jax.ShapeDtypeStruct((n,) + x.shape[1:], x.dtype),
        compiler_params=_cparams(("arbitrary",)), name="rowgather",
    )(src_idx, nvalid, x)


def _moe_kernel(te_ref, first_ref, nused_ref, x_ref, wg_ref, wu_ref, wd_ref, y_ref, wg_bf, wu_bf, wd_bf):
    i = pl.program_id(0)
    tile = x_ref.shape[0]

    @pl.when(first_ref[i] == 1)
    def _():
        wg_bf[...] = wg_ref[...].astype(BF16)
        wu_bf[...] = wu_ref[...].astype(BF16)
        wd_bf[...] = wd_ref[...].astype(BF16)

    @pl.when(i < nused_ref[0])
    def _():
        x = x_ref[...].reshape(tile, wg_bf.shape[0])
        gate = _dot(x, wg_bf[...])
        a = gate * _sigmoid(gate) * _dot(x, wu_bf[...])
        y_ref[...] = _dot(a.astype(BF16), wd_bf[...]).astype(BF16).reshape(y_ref.shape)

    @pl.when(i >= nused_ref[0])
    def _():
        y_ref[...] = jnp.zeros_like(y_ref)


def _moe_gemm(layer, te, first, nused, xs, w_gate, w_up, w_down, tile):
    p = xs.shape[0]
    d, f = w_gate.shape[-2:]
    nt = p // tile
    last = lambda i, nu: jnp.minimum(i, nu[0] - 1)
    grid_spec = pltpu.PrefetchScalarGridSpec(
        num_scalar_prefetch=3, grid=(nt,),
        in_specs=[
            pl.BlockSpec((tile,) + xs.shape[1:], lambda i, te, fi, nu: (last(i, nu), 0, 0)),
            pl.BlockSpec((None, None, d, f), lambda i, te, fi, nu: (layer, te[i], 0, 0)),
            pl.BlockSpec((None, None, d, f), lambda i, te, fi, nu: (layer, te[i], 0, 0)),
            pl.BlockSpec((None, None, f, d), lambda i, te, fi, nu: (layer, te[i], 0, 0)),
        ],
        out_specs=pl.BlockSpec((tile,) + xs.shape[1:], lambda i, te, fi, nu: (i, 0, 0)),
        scratch_shapes=[pltpu.VMEM((d, f), BF16), pltpu.VMEM((d, f), BF16), pltpu.VMEM((f, d), BF16)],
    )
    return pl.pallas_call(
        _moe_kernel, grid_spec=grid_spec,
        out_shape=jax.ShapeDtypeStruct(xs.shape, BF16),
        compiler_params=_cparams(("arbitrary",)), name="moe_gemm",
    )(te, first, nused, xs, w_gate, w_up, w_down)


def _moe(layer, x1r, route, counts, w_gate, w_up, w_down):
    tt = x1r.shape[0]
    tile = MOE_TILE
    tiles_per_gather = GATHER_ROWS // tile
    nt = -(-(-(-2 * tt // tile) + N_EXPERTS) // tiles_per_gather) * tiles_per_gather
    p = nt * tile
    e1 = route[0].astype(I32)
    e2 = route[1].astype(I32)
    cnt = counts[:, 0].astype(I32)
    padded = ((cnt + tile - 1) // tile) * tile
    ends = jnp.cumsum(padded)
    starts = ends - padded
    pos1 = starts[e1] + route[4].astype(I32)
    pos2 = starts[e2] + route[5].astype(I32)
    nused = (ends[-1] // tile).astype(I32)
    tile_id = jnp.minimum(jnp.arange(nt, dtype=I32), nused - 1)
    te = jnp.sum((ends[None, :] <= (tile_id * tile)[:, None]).astype(I32), axis=1)
    te = jnp.minimum(te, N_EXPERTS - 1)
    first = jnp.concatenate([jnp.ones((1,), I32), (te[1:] != te[:-1]).astype(I32)])
    tok = jnp.arange(tt, dtype=I32)
    src = jnp.concatenate([tok, tok])
    dst = jnp.concatenate([pos1, pos2])
    slot_token = jnp.zeros((p,), I32).at[dst].set(src)
    xs = _rowgather(slot_token, (nused * tile).reshape(1), x1r, GATHER_ROWS)
    ys = _moe_gemm(layer, te, first, nused.reshape(1), xs, w_gate, w_up, w_down, tile)
    back_rows = max(r for r in range(SUBLANES, GATHER_ROWS_CAP + 1, SUBLANES) if (2 * tt) % r == 0)
    back = _rowgather(dst, jnp.full((1,), 2 * tt, I32), ys, back_rows)
    return back.reshape((2, tt) + x1r.shape[1:])


def _finish_kernel(alpha, x1_ref, y_ref, gw_ref, p_ref, wpe_ref, wpg_ref, g2_ref, b2_ref, g3_ref, b3_ref,
                   h_ref, hb_ref):
    gw = gw_ref[...]
    y0 = y_ref[0].reshape(x1_ref.shape).astype(F32)
    y1 = y_ref[1].reshape(x1_ref.shape).astype(F32)
    moe = y0 * gw[:, 2:3] + y1 * gw[:, 3:4]
    x2 = _layer_norm(alpha * x1_ref[...] + moe, g2_ref[...], b2_ref[...])
    e = _dot(p_ref[...].astype(BF16), wpe_ref[...]) * _sigmoid(_dot(x2.astype(BF16), wpg_ref[...]))
    x3 = _layer_norm(alpha * x2 + e, g3_ref[...], b3_ref[...])
    h_ref[...] = x3
    hb_ref[...] = x3.astype(BF16)


def _finish(alpha, layer, x1, y2, route_t, p_l, pw, tm):
    tt, d = x1.shape
    lres = functools.partial(_layer_resident, layer)
    row = lambda w: pl.BlockSpec((tm, w), lambda i: (i, 0))
    return pl.pallas_call(
        functools.partial(_finish_kernel, alpha),
        grid=(tt // tm,),
        in_specs=[row(d), pl.BlockSpec((2, tm) + y2.shape[2:], lambda i: (0, i, 0, 0)), row(SUBLANES),
                  pl.BlockSpec((None, tm, p_l.shape[2]), lambda i: (layer, i, 0)),
                  lres(pw["w_pe"].shape[1:]), lres(pw["w_pg"].shape[1:]),
                  lres((1, d)), lres((1, d)), lres((1, d)), lres((1, d))],
        out_specs=(row(d), row(d)),
        out_shape=(jax.ShapeDtypeStruct((tt, d), F32), jax.ShapeDtypeStruct((tt, d), BF16)),
        compiler_params=_cparams(("parallel",)), name="finish",
    )(x1, y2, route_t, p_l, pw["w_pe"], pw["w_pg"], pw["ln2_g"], pw["ln2_b"], pw["ln3_g"], pw["ln3_b"])


def _pack_weights(w_in, q_norm_g, kv_norm_g, w_uq, w_ukv, gmlp_ln_g, gmlp_ln_b, gmlp_ws, gmlp_bs, fox_bf,
                  w_out, ln1_g, ln1_b, ln2_g, ln2_b, ln3_g, ln3_b, w_pe, w_pg):
    depth, d = w_in.shape[:2]
    splits = np.cumsum([Q_LORA, KV_LORA, ROPE_DIM, GM_WIDTH, GM_WIDTH, FOX_WIDTH, FOX_KV_WIDTH, FOX_KV_WIDTH])
    q_lat, kv_lat, k_r, u, v, fq, fk, fv, fl = jnp.split(w_in, [int(s) for s in splits], axis=2)
    w_in_p = jnp.concatenate(
        [q_lat, kv_lat, k_r, k_r, u, v, fq, fk, fv, fl, jnp.zeros((depth, d, LANES - FOX_HEADS), F32)], axis=2)
    uq = w_uq.reshape(depth, Q_LORA, MLA_HEADS, NOPE_DIM + ROPE_DIM)
    w_uq_p = jnp.concatenate([uq, uq[..., NOPE_DIM:]], axis=3).reshape(depth, Q_LORA, MLA_HEADS * Q_HEAD_COLS)
    ukv = w_ukv.reshape(depth, KV_LORA, MLA_HEADS, NOPE_DIM + MLA_V_DIM)
    w_uk_t = jnp.transpose(ukv[..., :NOPE_DIM], (0, 2, 3, 1))
    w_uv = jnp.transpose(ukv[..., NOPE_DIM:], (0, 2, 1, 3))
    tril = jnp.tril(jnp.ones((CHUNK, CHUNK), dtype=bool))
    gm_w = jnp.stack([jnp.where(tril, gmlp_ws, 0.0),
                      gmlp_ws[..., 0:1, 0:1] * jnp.eye(CHUNK, dtype=F32)], axis=1)
    bias_t = jnp.repeat(jnp.transpose(gmlp_bs, (0, 2, 1)), GM_CH, axis=2)
    bias_s = jnp.broadcast_to(jnp.repeat(gmlp_bs[:, :, 0], GM_CH, axis=1)[:, None, :], (depth, CHUNK, GM_WIDTH))
    bf = jnp.concatenate([fox_bf, jnp.zeros((depth, LANES - FOX_HEADS), F32)], axis=1)[:, None, :]
    r = lambda a: a[:, None, :]
    return dict(
        w_in=w_in_p.astype(BF16), q_norm_g=r(q_norm_g), kv_norm_g=r(kv_norm_g), w_uq=w_uq_p.astype(BF16),
        w_ukv=w_ukv.astype(BF16), w_uk_t=w_uk_t.astype(BF16), w_uv=w_uv.astype(BF16),
        gm_ln_g=r(gmlp_ln_g), gm_ln_b=r(gmlp_ln_b), gm_w=gm_w.astype(BF16),
        gm_bias=jnp.stack([bias_t, bias_s], axis=1),
        fox_bf=bf, w_out=w_out.astype(BF16), ln1_g=r(ln1_g), ln1_b=r(ln1_b), ln2_g=r(ln2_g), ln2_b=r(ln2_b),
        ln3_g=r(ln3_g), ln3_b=r(ln3_b), w_pe=w_pe.astype(BF16), w_pg=w_pg.astype(BF16))


def _row_tile(tt):
    for tm in range(ROW_TILE_CAP, 0, -CHUNK):
        if tt % tm == 0:
            return tm
    raise ValueError("stacked row count must be a multiple of the gMLP chunk")


def kernel(x_prompt, x_sample, cache_mla_kv, cache_mla_krope, cache_fox_k, cache_fox_v, cache_fox_logf, page_table, p_prompt, p_sample, w_in, q_norm_g, kv_norm_g, w_uq, w_ukv, gmlp_ln_g, gmlp_ln_b, gmlp_ws, gmlp_bs, fox_bf, w_out, ln1_g, ln1_b, ln2_g, ln2_b, ln3_g, ln3_b, router_w, router_b, w_e_gate, w_e_up, w_e_down, w_pe, w_pg):
    n_batch, t, d = x_prompt.shape
    db, dec_seq, _ = x_sample.shape
    depth = w_in.shape[0]
    n_pages = page_table.shape[1]
    page_size = cache_mla_kv.shape[2]
    past = n_pages * page_size
    assert dec_seq == 1 and t % CHUNK == 0 and db % CHUNK == 0
    tp = n_batch * t
    tt = tp + db
    tm = _row_tile(tt)
    tm_fin = tm // 2 if (tm // 2) % SUBLANES == 0 else tm
    blk = min(ATT_BLOCK, t)
    alpha = (2 * depth) ** 0.25

    pos = jnp.concatenate([jnp.tile(jnp.arange(t), n_batch), jnp.full((db,), past)]).astype(F32)
    half = ROPE_DIM // 2
    inv = ROPE_BASE ** (-jnp.arange(half, dtype=F32) / half)
    ang = pos[:, None] * inv[None, :]
    cos_t = jnp.tile(jnp.cos(ang), (1, LANES // half))
    sin_h = jnp.sin(ang)
    sin_t = jnp.tile(jnp.concatenate([-sin_h, sin_h], axis=1), (1, LANES // ROPE_DIM))

    cache_krt = jnp.transpose(cache_mla_krope, (0, 1, 3, 2))
    n_pool = cache_fox_k.shape[1]
    cache_kt = jnp.transpose(cache_fox_k, (0, 1, 3, 4, 2)).reshape(depth, n_pool, FOX_KV_WIDTH, page_size)
    cache_vt = jnp.transpose(cache_fox_v, (0, 1, 3, 4, 2)).reshape(depth, n_pool, FOX_KV_WIDTH, page_size)
    cache_lft = jnp.transpose(cache_fox_logf, (0, 1, 3, 2))

    tri_lower = jnp.tril(jnp.ones((CHUNK, CHUNK), F32)).astype(BF16)
    tri_before = jnp.triu(jnp.ones((tm, tm), F32), k=1).astype(BF16)
    rwt = router_w.T
    rb = router_b[:, None]

    h = jnp.concatenate([x_prompt.reshape(tp, d), x_sample.reshape(db, d)], axis=0)
    hb = h.astype(BF16)
    outs = [[] for _ in range(11)]
    p_all = jnp.concatenate([p_prompt.reshape(depth, tp, -1), p_sample.reshape(depth, db, -1)], axis=1)
    pw = _pack_weights(w_in, q_norm_g, kv_norm_g, w_uq, w_ukv, gmlp_ln_g, gmlp_ln_b, gmlp_ws, gmlp_bs,
                       fox_bf, w_out, ln1_g, ln1_b, ln2_g, ln2_b, ln3_g, ln3_b, w_pe, w_pg)
    for l in range(depth):
        q, ckv, kr, kvu, gv, go, fq3, fk, fv, fk3, fv3, lf = _inproj(l, hb, pw, cos_t, sin_t, tp // CHUNK, tm)

        o_mla_p = _mla_prompt(q, kvu, kr, n_batch, t, blk)
        c = _fox_cumsum(lf, tri_lower, n_batch, t)[:, :FOX_HEADS].reshape(n_batch, t, FOX_HEADS)
        c_hm = jnp.transpose(c, (0, 2, 1))
        o_fox_p = _fox_prompt(fq3, fk3, fv3, c_hm[..., None], c_hm[:, :, None, :], n_batch, t, blk)

        q_s = q[tp:].reshape(db, MLA_HEADS, Q_HEAD_COLS)
        qa = _heads_mm(q_s[:, :, :NOPE_DIM].reshape(db, MLA_HEADS * NOPE_DIM), pw["w_uk_t"][l], BF16, "mla_absorb_q")
        o_lat = _mla_decode(l, page_table, qa.reshape(db, MLA_HEADS, KV_LORA), q_s[:, :, NOPE_DIM:],
                            ckv[tp:, None, :], kr[tp:, None, :], cache_mla_kv, cache_krt)
        o_mla_s = _heads_mm(o_lat.reshape(db, MLA_HEADS * KV_LORA), pw["w_uv"][l], BF16, "mla_absorb_v")
        fq_s = fq3[:, tp:, :FOX_HEAD_DIM]
        zeros_q = jnp.zeros_like(fq_s)
        kvh = (jnp.arange(FOX_HEADS) // FOX_GROUP)[:, None, None]
        qbd = jnp.concatenate([jnp.where(kvh == 0, fq_s, zeros_q), jnp.where(kvh == 1, fq_s, zeros_q)], axis=2)
        o8 = _fox_decode(l, page_table, jnp.transpose(qbd, (1, 0, 2)), fk[tp:, None, :], fv[tp:, None, :],
                         lf[tp:, :FOX_HEADS, None], cache_kt, cache_vt, cache_lft)
        o8 = o8.reshape(db, FOX_KV_HEADS, FOX_GROUP, FOX_KV_HEADS, FOX_HEAD_DIM)
        o_fox_s = jnp.stack([o8[:, n, :, n, :] for n in range(FOX_KV_HEADS)], axis=1).reshape(db, FOX_WIDTH)

        o_mla = jnp.concatenate([o_mla_p, o_mla_s], axis=0)
        o_fox = jnp.concatenate([o_fox_p, o_fox_s.astype(BF16)], axis=0)
        x1, x1r, route, counts = _outproj(alpha, l, h, o_mla, go, o_fox, pw, rwt, rb, tri_before, tm)
        y2 = _moe(l, x1r, route, counts, w_e_gate, w_e_up, w_e_down)
        h, hb = _finish(alpha, l, x1, y2, route.T, p_all, pw, tm_fin)

        leaves = (ckv[:tp].reshape(n_batch, t, KV_LORA), kr[:tp, :ROPE_DIM].reshape(n_batch, t, ROPE_DIM),
                  fk[:tp].reshape(n_batch, t, FOX_KV_HEADS, FOX_HEAD_DIM),
                  fv[:tp].reshape(n_batch, t, FOX_KV_HEADS, FOX_HEAD_DIM),
                  lf[:tp, :FOX_HEADS].reshape(n_batch, t, FOX_HEADS),
                  ckv[tp:].reshape(db, 1, KV_LORA), kr[tp:, :ROPE_DIM].reshape(db, 1, ROPE_DIM),
                  fk[tp:].reshape(db, 1, FOX_KV_HEADS, FOX_HEAD_DIM), fv[tp:].reshape(db, 1, FOX_KV_HEADS, FOX_HEAD_DIM),
                  lf[tp:, :FOX_HEADS].reshape(db, 1, FOX_HEADS), gv[tp:].reshape(db, 1, GM_WIDTH))
        for acc, leaf in zip(outs, leaves):
            acc.append(leaf)
    return (h[:tp].reshape(n_batch, t, d), h[tp:].reshape(db, 1, d)) + tuple(jnp.stack(o) for o in outs)
```
